```python
import jax
import jax.numpy as jnp
from jax import lax
import numpy as np

D_MODEL = 2048
BATCH = 2
SEQ = 8192
DEPTH = 4

CHUNK = 64
N_MIXERS = 3
RMS_EPS = 1e-6

GLA_HEADS = 4
GLA_KEY_DIM = D_MODEL // 2
GLA_VAL_DIM = D_MODEL
GLA_HEAD_K = GLA_KEY_DIM // GLA_HEADS
GLA_HEAD_V = GLA_VAL_DIM // GLA_HEADS
GLA_GATE_RANK = 16
GLA_GATE_TAU = 16.0
GLA_IN = 2 * GLA_KEY_DIM + 2 * GLA_VAL_DIM + GLA_GATE_RANK

LRU_WIDTH = D_MODEL
LRU_BLOCKS = LRU_WIDTH // 256
LRU_BLOCK_W = LRU_WIDTH // LRU_BLOCKS
LRU_CONV = 4
LRU_C = 8.0

RET_HEADS = 8
RET_HEAD_K = D_MODEL // RET_HEADS
RET_HEAD_V = 2 * RET_HEAD_K
RET_KEY_DIM = RET_HEADS * RET_HEAD_K
RET_VAL_DIM = RET_HEADS * RET_HEAD_V
RET_IN = 2 * RET_KEY_DIM + 2 * RET_VAL_DIM
ROPE_BASE = 10000.0

FFN_DIM = 3 * D_MODEL
FFN_CONV = 3

N_GLA = (DEPTH + 2) // 3
N_LRU = (DEPTH + 1) // 3
N_RET = DEPTH // 3

kernel_name = "hybrid_gla_rglru_retention_convffn"

F32 = jnp.float32


def rms_norm(x, w):
    xf = x.astype(F32)
    y = xf * lax.rsqrt(jnp.mean(xf * xf, axis=-1, keepdims=True) + RMS_EPS)
    return (y * w.astype(F32)).astype(x.dtype)


def head_rms_norm(o, w):
    B, S, H, d = o.shape
    o = o * lax.rsqrt(jnp.mean(o * o, axis=-1, keepdims=True) + RMS_EPS)
    return o.reshape(B, S, H * d) * w.astype(F32)


def head_layer_norm(o, w):
    B, S, H, d = o.shape
    mu = jnp.mean(o, axis=-1, keepdims=True)
    oc = o - mu
    o = oc * lax.rsqrt(jnp.mean(oc * oc, axis=-1, keepdims=True) + RMS_EPS)
    return o.reshape(B, S, H * d) * w.astype(F32)


def causal_depthwise_conv(x, w, b):
    K = w.shape[0]
    y = lax.conv_general_dilated(
        x, w[:, None, :].astype(x.dtype), window_strides=(1,), padding=((K - 1, 0),),
        dimension_numbers=("NWC", "WIO", "NWC"), feature_group_count=x.shape[-1])
    return y + b.astype(x.dtype)


def to_chunks(t, heads):
    B, S, _ = t.shape
    return t.reshape(B, S // CHUNK, CHUNK, heads, -1).transpose(0, 3, 1, 2, 4)


def from_chunks(t):
    B, H, N, C, d = t.shape
    return t.transpose(0, 2, 3, 1, 4).reshape(B, N * C, H, d)


def chunk_state_readout(q_in, k_in, v, decay):
    B, H, N, C, dk = q_in.shape
    dv = v.shape[-1]

    def step(S, inp):
        qn, kn, vn, dn = inp
        o = jnp.einsum("bhcd,bhde->bhce", qn, S)
        S = dn[..., None] * S + jnp.einsum("bhcd,bhce->bhde", kn, vn)
        return S, o

    xs = (jnp.moveaxis(q_in, 2, 0), jnp.moveaxis(k_in, 2, 0),
          jnp.moveaxis(v, 2, 0), jnp.moveaxis(decay, 2, 0))
    S0 = jnp.zeros((B, H, dk, dv), F32)
    _, o = lax.scan(step, S0, xs)
    return jnp.moveaxis(o, 0, 2)


def rope(x):
    S, d = x.shape[1], x.shape[-1]
    half = d // 2
    inv = ROPE_BASE ** (-jnp.arange(half, dtype=F32) / half)
    ang = jnp.arange(S, dtype=F32)[:, None] * inv[None, :]
    cos = jnp.cos(ang)[None, :, None, :]
    sin = jnp.sin(ang)[None, :, None, :]
    x1, x2 = x[..., :half], x[..., half:]
    return jnp.concatenate([x1 * cos - x2 * sin, x2 * cos + x1 * sin], axis=-1)


def gla_mixer(h, w_in, w_gk, b_gk, norm_w, w_out):
    B, S, _ = h.shape
    proj = h @ w_in
    q, k, v, g, z = jnp.split(
        proj, [GLA_KEY_DIM, 2 * GLA_KEY_DIM, 2 * GLA_KEY_DIM + GLA_VAL_DIM,
               2 * GLA_KEY_DIM + 2 * GLA_VAL_DIM], axis=-1)
    log_a = jax.nn.log_sigmoid((z @ w_gk + b_gk).astype(F32)) / GLA_GATE_TAU
    q = to_chunks(q.astype(F32) * (GLA_HEAD_K ** -0.5), GLA_HEADS)
    k = to_chunks(k.astype(F32), GLA_HEADS)
    v = to_chunks(v.astype(F32), GLA_HEADS)
    b = jnp.cumsum(to_chunks(log_a, GLA_HEADS), axis=3)
    b_last = b[:, :, :, -1:, :]
    eb, enb = jnp.exp(b), jnp.exp(-b)
    q_dec, k_grow = q * eb, k * enb
    q_grow, k_dec = q * enb, k * eb
    idx = jnp.arange(CHUNK)
    lower = idx[:, None] >= idx[None, :]
    scores = jnp.where(lower,
                       jnp.einsum("bhntd,bhnsd->bhnts", q_dec, k_grow),
                       jnp.einsum("bhntd,bhnsd->bhnts", q_grow, k_dec))
    o = jnp.einsum("bhnts,bhnse->bhnte", scores, v)
    o = o + chunk_state_readout(q_dec, k * jnp.exp(b_last - b), v, jnp.exp(b_last[:, :, :, 0, :]))
    o = head_rms_norm(from_chunks(o), norm_w) * jax.nn.silu(g.astype(F32))
    return o.astype(h.dtype) @ w_out


def _lin_rec_combine(c1, c2):
    a1, b1 = c1
    a2, b2 = c2
    return a1 * a2, a2 * b1 + b2


def rglru_mixer(h, w_in, conv_w, conv_b, w_ga, b_ga, w_gx, b_gx, lam, w_out):
    B, S, _ = h.shape
    xb, yb = jnp.split(h @ w_in, 2, axis=-1)
    yb = jax.nn.gelu(yb)
    xb = causal_depthwise_conv(xb, conv_w, conv_b)
    xblk = xb.reshape(B, S, LRU_BLOCKS, LRU_BLOCK_W)
    r = jax.nn.sigmoid(jnp.einsum("bsnc,ncd->bsnd", xblk, w_ga).reshape(B, S, LRU_WIDTH) + b_ga)
    i = jax.nn.sigmoid(jnp.einsum("bsnc,ncd->bsnd", xblk, w_gx).reshape(B, S, LRU_WIDTH) + b_gx)
    log_a = -LRU_C * r.astype(F32) * jax.nn.softplus(-lam.astype(F32))
    a = jnp.exp(log_a)
    u = xb.astype(F32) * i.astype(F32) * jnp.sqrt(-jnp.expm1(2.0 * log_a))
    _, hs = lax.associative_scan(_lin_rec_combine, (a, u), axis=1)
    return (hs * yb.astype(F32)).astype(h.dtype) @ w_out


def retention_mixer(h, w_in, norm_w, w_out):
    B, S, _ = h.shape
    q, k, v, g = jnp.split(h @ w_in, [RET_KEY_DIM, 2 * RET_KEY_DIM, 2 * RET_KEY_DIM + RET_VAL_DIM], axis=-1)
    q = rope(q.astype(F32).reshape(B, S, RET_HEADS, RET_HEAD_K)).reshape(B, S, RET_KEY_DIM)
    k = rope(k.astype(F32).reshape(B, S, RET_HEADS, RET_HEAD_K)).reshape(B, S, RET_KEY_DIM) * (RET_HEAD_K ** -0.5)
    q = to_chunks(q, RET_HEADS)
    k = to_chunks(k, RET_HEADS)
    v = to_chunks(v.astype(F32), RET_HEADS)
    log_g = jnp.log(1.0 - jnp.exp2(-5.0 - jnp.arange(RET_HEADS, dtype=F32)))
    pos = jnp.arange(CHUNK, dtype=F32)
    dist = jnp.abs(pos[:, None] - pos[None, :])
    D = jnp.exp(log_g[:, None, None] * dist)
    scores = jnp.einsum("bhntd,bhnsd->bhnts", q, k) * D[None, :, None]
    o = jnp.einsum("bhnts,bhnse->bhnte", scores, v)
    xi = jnp.exp(log_g[:, None] * (pos + 1.0))[None, :, None, :, None]
    zeta = jnp.exp(log_g[:, None] * (CHUNK - 1.0 - pos))[None, :, None, :, None]
    n_chunks = S // CHUNK
    decay = jnp.broadcast_to(jnp.exp(log_g * CHUNK)[None, :, None, None], (B, RET_HEADS, n_chunks, RET_HEAD_K))
    o = o + chunk_state_readout(q * xi, k * zeta, v, decay)
    o = head_layer_norm(from_chunks(o), norm_w) * jax.nn.silu(g.astype(F32))
    return o.astype(h.dtype) @ w_out


def conv_ffn(h, w_up, conv_w, conv_b, w_down):
    u = causal_depthwise_conv(h @ w_up, conv_w, conv_b)
    gate, val = jnp.split(u, 2, axis=-1)
    return (jax.nn.gelu(gate) * val) @ w_down


def setup_inputs(seed: int = 0) -> dict:
    key = jax.random.key(seed)
    ks = iter(jax.random.split(key, 32))

    def nrm(shape, fan_in):
        return jax.random.normal(next(ks), shape, F32) * (fan_in ** -0.5)

    def gain(shape):
        return 1.0 + 0.02 * jax.random.normal(next(ks), shape, F32)

    def bias(shape):
        return 0.01 * jax.random.normal(next(ks), shape, F32)

    x = jax.random.normal(next(ks), (BATCH, SEQ, D_MODEL), F32)
    norm_mix_w = gain((DEPTH, D_MODEL))
    norm_ffn_w = gain((DEPTH, D_MODEL))
    norm_out_w = gain((D_MODEL,))

    gla_w_in = nrm((N_GLA, D_MODEL, GLA_IN), D_MODEL)
    gla_w_gk = nrm((N_GLA, GLA_GATE_RANK, GLA_KEY_DIM), GLA_GATE_RANK)
    gla_b_gk = bias((N_GLA, GLA_KEY_DIM))
    gla_norm_w = gain((N_GLA, GLA_VAL_DIM))
    gla_w_out = nrm((N_GLA, GLA_VAL_DIM, D_MODEL), GLA_VAL_DIM)

    lru_w_in = nrm((N_LRU, D_MODEL, 2 * LRU_WIDTH), D_MODEL)
    lru_conv_w = nrm((N_LRU, LRU_CONV, LRU_WIDTH), LRU_CONV)
    lru_conv_b = bias((N_LRU, LRU_WIDTH))
    lru_w_ga = nrm((N_LRU, LRU_BLOCKS, LRU_BLOCK_W, LRU_BLOCK_W), LRU_BLOCK_W)
    lru_b_ga = bias((N_LRU, LRU_WIDTH))
    lru_w_gx = nrm((N_LRU, LRU_BLOCKS, LRU_BLOCK_W, LRU_BLOCK_W), LRU_BLOCK_W)
    lru_b_gx = bias((N_LRU, LRU_WIDTH))
    a_pow = jax.random.uniform(next(ks), (N_LRU, LRU_WIDTH), F32, minval=0.9, maxval=0.999)
    s = a_pow ** (1.0 / LRU_C)
    lru_lambda = jnp.log(s) - jnp.log1p(-s)
    lru_w_out = nrm((N_LRU, LRU_WIDTH, D_MODEL), LRU_WIDTH)

    ret_w_in = nrm((N_RET, D_MODEL, RET_IN), D_MODEL)
    ret_norm_w = gain((N_RET, RET_VAL_DIM))
    ret_w_out = nrm((N_RET, RET_VAL_DIM, D_MODEL), RET_VAL_DIM)

    ffn_w_up = nrm((DEPTH, D_MODEL, 2 * FFN_DIM), D_MODEL)
    ffn_conv_w = nrm((DEPTH, FFN_CONV, 2 * FFN_DIM), FFN_CONV)
    ffn_conv_b = bias((DEPTH, 2 * FFN_DIM))
    ffn_w_down = nrm((DEPTH, FFN_DIM, D_MODEL), FFN_DIM)

    return {
        "x": x, "norm_mix_w": norm_mix_w, "norm_ffn_w": norm_ffn_w, "norm_out_w": norm_out_w,
        "gla_w_in": gla_w_in, "gla_w_gk": gla_w_gk, "gla_b_gk": gla_b_gk,
        "gla_norm_w": gla_norm_w, "gla_w_out": gla_w_out,
        "lru_w_in": lru_w_in, "lru_conv_w": lru_conv_w, "lru_conv_b": lru_conv_b,
        "lru_w_ga": lru_w_ga, "lru_b_ga": lru_b_ga, "lru_w_gx": lru_w_gx, "lru_b_gx": lru_b_gx,
        "lru_lambda": lru_lambda, "lru_w_out": lru_w_out,
        "ret_w_in": ret_w_in, "ret_norm_w": ret_norm_w, "ret_w_out": ret_w_out,
        "ffn_w_up": ffn_w_up, "ffn_conv_w": ffn_conv_w, "ffn_conv_b": ffn_conv_b,
        "ffn_w_down": ffn_w_down,
    }


def reference(x, norm_mix_w, norm_ffn_w, norm_out_w,
              gla_w_in, gla_w_gk, gla_b_gk, gla_norm_w, gla_w_out,
              lru_w_in, lru_conv_w, lru_conv_b, lru_w_ga, lru_b_ga, lru_w_gx, lru_b_gx,
              lru_lambda, lru_w_out,
              ret_w_in, ret_norm_w, ret_w_out,
              ffn_w_up, ffn_conv_w, ffn_conv_b, ffn_w_down):
    for i in range(DEPTH):
        kind, j = i % N_MIXERS, i // N_MIXERS
        h = rms_norm(x, norm_mix_w[i])
        if kind == 0:
            y = gla_mixer(h, gla_w_in[j], gla_w_gk[j], gla_b_gk[j], gla_norm_w[j], gla_w_out[j])
        elif kind == 1:
            y = rglru_mixer(h, lru_w_in[j], lru_conv_w[j], lru_conv_b[j], lru_w_ga[j], lru_b_ga[j],
                            lru_w_gx[j], lru_b_gx[j], lru_lambda[j], lru_w_out[j])
        else:
            y = retention_mixer(h, ret_w_in[j], ret_norm_w[j], ret_w_out[j])
        x = x + y.astype(x.dtype)
        x = x + conv_ffn(rms_norm(x, norm_ffn_w[i]), ffn_w_up[i], ffn_conv_w[i], ffn_conv_b[i],
                         ffn_w_down[i]).astype(x.dtype)
    return rms_norm(x, norm_out_w)
```

```python
import functools
import math

import jax
import jax.numpy as jnp
from jax import lax
from jax.experimental import pallas as pl
from jax.experimental.pallas import tpu as pltpu

F32 = jnp.float32
BF16 = jnp.bfloat16

CHUNK = 64
RMS_EPS = 1e-6
N_MIXERS = 3
GLA_HEADS = 4
GLA_GATE_TAU = 16.0
LRU_BLOCK_W = 256
LRU_C = 8.0
RET_HEADS = 8
ROPE_BASE = 10000.0

LANES = 128
SUBLANES = 8
VMEM_LIMIT = 56 * 1024 * 1024


def _params(sem):
    return pltpu.CompilerParams(dimension_semantics=sem, vmem_limit_bytes=VMEM_LIMIT)


def _sigmoid(x):
    return 1.0 / (1.0 + jnp.exp(-x))


def _gelu_tanh(x):
    c = math.sqrt(2.0 / math.pi)
    return x * (0.5 * (1.0 + jnp.tanh(c * (x + 0.044715 * (x * x * x)))))


def _softplus(x):
    return jnp.maximum(x, 0.0) + jnp.log1p(jnp.exp(-jnp.abs(x)))


def _dot(a, b):
    return jnp.dot(a, b, preferred_element_type=F32)


def _dot_nt(a, b):
    return lax.dot_general(a, b, (((1,), (1,)), ((), ())), preferred_element_type=F32)


def _dot_tn(a, b):
    return lax.dot_general(a, b, (((0,), (0,)), ((), ())), preferred_element_type=F32)


def _norm_proj_kernel(x_ref, nw_ref, w_ref, *rest, has_extra):
    if has_extra:
        we_ref, o_ref, oe_ref, h_ref = rest
    else:
        o_ref, h_ref = rest

    @pl.when(pl.program_id(1) == 0)
    def _():
        x = x_ref[...]
        ms = jnp.mean(x * x, axis=-1, keepdims=True)
        h_ref[...] = (x * lax.rsqrt(ms + RMS_EPS) * nw_ref[...]).astype(BF16)
        if has_extra:
            oe_ref[...] = _dot(h_ref[...], we_ref[...]).astype(oe_ref.dtype)

    o_ref[...] = _dot(h_ref[...], w_ref[...]).astype(o_ref.dtype)


def _norm_proj(x, nw, w, w_extra=None, *, tm, tn, name):
    M, D = x.shape
    N = w.shape[1]
    has_extra = w_extra is not None
    in_specs = [
        pl.BlockSpec((tm, D), lambda i, j: (i, 0)),
        pl.BlockSpec((1, D), lambda i, j: (0, 0)),
        pl.BlockSpec((D, tn), lambda i, j: (0, j)),
    ]
    args = [x, nw.reshape(1, D), w]
    out_shape = [jax.ShapeDtypeStruct((M, N), BF16)]
    out_specs = [pl.BlockSpec((tm, tn), lambda i, j: (i, j))]
    if has_extra:
        ne = w_extra.shape[1]
        in_specs.append(pl.BlockSpec((D, ne), lambda i, j: (0, 0)))
        args.append(w_extra)
        out_shape.append(jax.ShapeDtypeStruct((M, ne), BF16))
        out_specs.append(pl.BlockSpec((tm, ne), lambda i, j: (i, 0)))
    res = pl.pallas_call(
        functools.partial(_norm_proj_kernel, has_extra=has_extra),
        grid=(M // tm, N // tn),
        in_specs=in_specs,
        out_specs=out_specs,
        out_shape=out_shape,
        scratch_shapes=[pltpu.VMEM((tm, D), BF16)],
        compiler_params=_params(("parallel", "arbitrary")),
        name=name,
    )(*args)
    return res if has_extra else res[0]


def _proj_res_kernel(y_ref, w_ref, x_ref, o_ref):
    o_ref[...] = x_ref[...] + _dot(y_ref[...], w_ref[...])


def _proj_res(y, w, x, *, tm, tn, name):
    M, K = y.shape
    N = w.shape[1]
    return pl.pallas_call(
        _proj_res_kernel,
        grid=(M // tm, N // tn),
        in_specs=[
            pl.BlockSpec((tm, K), lambda i, j: (i, 0)),
            pl.BlockSpec((K, tn), lambda i, j: (0, j)),
            pl.BlockSpec((tm, tn), lambda i, j: (i, j)),
        ],
        out_specs=pl.BlockSpec((tm, tn), lambda i, j: (i, j)),
        out_shape=jax.ShapeDtypeStruct((M, N), F32),
        compiler_params=_params(("parallel", "arbitrary")),
        name=name,
    )(y, w, x)


def _ffn_kernel(x_ref, nw_ref, wg_ref, wv_ref, cwg_ref, cwv_ref, cbg_ref, cbv_ref, wd_ref, *rest,
                tiles_per_seq, final_norm):
    if final_norm:
        fnw_ref, o_ref, h_ref, acc_ref, carg_ref, carv_ref = rest
    else:
        o_ref, h_ref, acc_ref, carg_ref, carv_ref = rest
    i = pl.program_id(0)
    j = pl.program_id(1)
    nj = pl.num_programs(1)
    tm = x_ref.shape[0]

    @pl.when(j == 0)
    def _():
        x = x_ref[...]
        ms = jnp.mean(x * x, axis=-1, keepdims=True)
        h_ref[...] = (x * lax.rsqrt(ms + RMS_EPS) * nw_ref[...]).astype(BF16)

    seq_start = (i % tiles_per_seq) == 0
    h = h_ref[...]

    def conv_half(w_ref, cw_ref, cb_ref, car_ref):
        u = _dot(h, w_ref[...])
        prev = jnp.where(seq_start, 0.0, car_ref[j])
        car_ref[j] = u[tm - SUBLANES:, :]
        ext = jnp.concatenate([prev, u], axis=0)
        cw = cw_ref[...]
        return (cw[0:1, :] * ext[SUBLANES - 2:SUBLANES - 2 + tm, :]
                + cw[1:2, :] * ext[SUBLANES - 1:SUBLANES - 1 + tm, :]
                + cw[2:3, :] * u + cb_ref[...])

    gate = conv_half(wg_ref, cwg_ref, cbg_ref, carg_ref)
    val = conv_half(wv_ref, cwv_ref, cbv_ref, carv_ref)
    act = (_gelu_tanh(gate) * val).astype(BF16)
    part = _dot(act, wd_ref[...])

    @pl.when(j == 0)
    def _():
        acc_ref[...] = part

    @pl.when(j > 0)
    def _():
        acc_ref[...] += part

    @pl.when(j == nj - 1)
    def _():
        y = x_ref[...] + acc_ref[...]
        if final_norm:
            ms = jnp.mean(y * y, axis=-1, keepdims=True)
            y = y * lax.rsqrt(ms + RMS_EPS) * fnw_ref[...]
        o_ref[...] = y


def _ffn(x, nw, w_up, conv_w, conv_b, w_down, final_nw, *, seq_len, tm, tf, name):
    M, D = x.shape
    F = w_down.shape[0]
    nj = F // tf
    final_norm = final_nw is not None
    in_specs = [
        pl.BlockSpec((tm, D), lambda i, j: (i, 0)),
        pl.BlockSpec((1, D), lambda i, j: (0, 0)),
        pl.BlockSpec((D, tf), lambda i, j: (0, j)),
        pl.BlockSpec((D, tf), lambda i, j: (0, nj + j)),
        pl.BlockSpec((conv_w.shape[0], tf), lambda i, j: (0, j)),
        pl.BlockSpec((conv_w.shape[0], tf), lambda i, j: (0, nj + j)),
        pl.BlockSpec((1, tf), lambda i, j: (0, j)),
        pl.BlockSpec((1, tf), lambda i, j: (0, nj + j)),
        pl.BlockSpec((tf, D), lambda i, j: (j, 0)),
    ]
    cb = conv_b.reshape(1, 2 * F)
    args = [x, nw.reshape(1, D), w_up, w_up, conv_w, conv_w, cb, cb, w_down]
    if final_norm:
        in_specs.append(pl.BlockSpec((1, D), lambda i, j: (0, 0)))
        args.append(final_nw.reshape(1, D))
    return pl.pallas_call(
        functools.partial(_ffn_kernel, tiles_per_seq=seq_len // tm, final_norm=final_norm),
        grid=(M // tm, nj),
        in_specs=in_specs,
        out_specs=pl.BlockSpec((tm, D), lambda i, j: (i, 0)),
        out_shape=jax.ShapeDtypeStruct((M, D), F32),
        scratch_shapes=[
            pltpu.VMEM((tm, D), BF16),
            pltpu.VMEM((tm, D), F32),
            pltpu.VMEM((nj, SUBLANES, tf), F32),
            pltpu.VMEM((nj, SUBLANES, tf), F32),
        ],
        compiler_params=_params(("arbitrary", "arbitrary")),
        name=name,
    )(*args)


def _chunk_masks():
    r = lax.broadcasted_iota(jnp.int32, (CHUNK, CHUNK), 0)
    c = lax.broadcasted_iota(jnp.int32, (CHUNK, CHUNK), 1)
    return r, c


def _gla_kernel(q_ref, k_ref, v_ref, g_ref, z_ref, wgk_ref, bgk_ref, nw_ref, o_ref, st_ref, *,
                n_chunks, q_scale):
    @pl.when(pl.program_id(2) == 0)
    def _():
        st_ref[...] = jnp.zeros_like(st_ref)

    r, c = _chunk_masks()
    lower = r >= c
    tri = lower.astype(BF16)
    wgk = wgk_ref[...]
    bgk = bgk_ref[...]
    nw = nw_ref[...]

    for ci in range(n_chunks):
        sl = pl.ds(ci * CHUNK, CHUNK)
        pre = _dot(z_ref[sl, :], wgk) + bgk
        log_a = -_softplus(-pre) / GLA_GATE_TAU
        la_hi = log_a.astype(BF16)
        la_lo = (log_a - la_hi.astype(F32)).astype(BF16)
        b = _dot(tri, la_hi) + _dot(tri, la_lo)
        b_last = b[CHUNK - 1:CHUNK, :]
        eb = jnp.exp(b)
        enb = jnp.exp(-b)
        q = q_ref[sl, :].astype(F32) * q_scale
        k = k_ref[sl, :].astype(F32)
        v = v_ref[sl, :]
        q_dec = (q * eb).astype(BF16)
        k_grow = (k * enb).astype(BF16)
        q_grow = (q * enb).astype(BF16)
        k_dec = (k * eb).astype(BF16)
        scores = jnp.where(lower, _dot_nt(q_dec, k_grow), _dot_nt(q_grow, k_dec))
        st = st_ref[...]
        o = _dot(scores.astype(BF16), v) + _dot_nt(q_dec, st.astype(BF16))
        k_in = (k * jnp.exp(b_last - b)).astype(BF16)
        st_ref[...] = jnp.exp(b_last) * st + _dot_tn(v, k_in)
        o = o * lax.rsqrt(jnp.mean(o * o, axis=-1, keepdims=True) + RMS_EPS) * nw
        g = g_ref[sl, :].astype(F32)
        o_ref[sl, :] = (o * (g * _sigmoid(g))).astype(o_ref.dtype)


def _gla_core(proj, z, wgk, bgk, norm_w, *, batch, seq_len, tc, name):
    M = proj.shape[0]
    kd = wgk.shape[1]
    vd = norm_w.shape[0]
    H = GLA_HEADS
    dk, dv = kd // H, vd // H
    nt = seq_len // tc
    row = lambda b, h, n: b * nt + n
    return pl.pallas_call(
        functools.partial(_gla_kernel, n_chunks=tc // CHUNK, q_scale=dk ** -0.5),
        grid=(batch, H, nt),
        in_specs=[
            pl.BlockSpec((tc, dk), lambda b, h, n: (row(b, h, n), h)),
            pl.BlockSpec((tc, dk), lambda b, h, n: (row(b, h, n), kd // dk + h)),
            pl.BlockSpec((tc, dv), lambda b, h, n: (row(b, h, n), 2 * kd // dv + h)),
            pl.BlockSpec((tc, dv), lambda b, h, n: (row(b, h, n), (2 * kd + vd) // dv + h)),
            pl.BlockSpec((tc, z.shape[1]), lambda b, h, n: (row(b, h, n), 0)),
            pl.BlockSpec((wgk.shape[0], dk), lambda b, h, n: (0, h)),
            pl.BlockSpec((1, dk), lambda b, h, n: (0, h)),
            pl.BlockSpec((1, dv), lambda b, h, n: (0, h)),
        ],
        out_specs=pl.BlockSpec((tc, dv), lambda b, h, n: (row(b, h, n), h)),
        out_shape=jax.ShapeDtypeStruct((M, vd), BF16),
        scratch_shapes=[pltpu.VMEM((dv, dk), F32)],
        compiler_params=_params(("arbitrary", "arbitrary", "arbitrary")),
        name=name,
    )(proj, proj, proj, proj, z, wgk, bgk.reshape(1, kd), norm_w.reshape(1, vd))


def _rope_kernel(cos_ref, sin_ref):
    tt, half = cos_ref.shape
    pos = (pl.program_id(0) * tt + lax.broadcasted_iota(jnp.int32, (tt, half), 0)).astype(F32)
    jj = lax.broadcasted_iota(jnp.int32, (tt, half), 1).astype(F32)
    inv = jnp.exp(-(jj / half) * math.log(ROPE_BASE))
    ang = pos * inv
    cos_ref[...] = jnp.cos(ang)
    sin_ref[...] = jnp.sin(ang)


def _rope_tables(seq_len, half, *, tt):
    return pl.pallas_call(
        _rope_kernel,
        grid=(seq_len // tt,),
        out_specs=[pl.BlockSpec((tt, half), lambda i: (i, 0))] * 2,
        out_shape=[jax.ShapeDtypeStruct((seq_len, half), F32)] * 2,
        compiler_params=_params(("parallel",)),
        name="rope_tables",
    )()


def _ret_kernel(q_ref, k_ref, v_ref, g_ref, cos_ref, sin_ref, nw_ref, o_ref, st_ref, *,
                n_chunks, k_scale):
    @pl.when(pl.program_id(2) == 0)
    def _():
        st_ref[...] = jnp.zeros_like(st_ref)

    dk = q_ref.shape[1]
    half = dk // 2
    head = pl.program_id(1).astype(F32)

    def log_gamma(shape):
        return jnp.log(1.0 - jnp.exp2(-5.0 - jnp.full(shape, head, F32)))

    r, c = _chunk_masks()
    dist = jnp.abs(r - c).astype(F32)
    dmat = jnp.exp(log_gamma((CHUNK, CHUNK)) * dist)
    pos = lax.broadcasted_iota(jnp.int32, (CHUNK, dk), 0).astype(F32)
    lg = log_gamma((CHUNK, dk))
    xi = jnp.exp(lg * (pos + 1.0))
    zeta = jnp.exp(lg * (CHUNK - 1.0 - pos))
    decay = jnp.exp(log_gamma((1, dk)) * float(CHUNK))
    nw = nw_ref[...]

    def rope(x, cos, sin):
        x1, x2 = x[:, :half], x[:, half:]
        return jnp.concatenate([x1 * cos - x2 * sin, x2 * cos + x1 * sin], axis=-1)

    for ci in range(n_chunks):
        sl = pl.ds(ci * CHUNK, CHUNK)
        cos = cos_ref[sl, :]
        sin = sin_ref[sl, :]
        q = rope(q_ref[sl, :].astype(F32), cos, sin)
        k = rope(k_ref[sl, :].astype(F32), cos, sin) * k_scale
        v = v_ref[sl, :]
        scores = _dot_nt(q.astype(BF16), k.astype(BF16)) * dmat
        st = st_ref[...]
        o = _dot(scores.astype(BF16), v) + _dot_nt((q * xi).astype(BF16), st.astype(BF16))
        st_ref[...] = decay * st + _dot_tn(v, (k * zeta).astype(BF16))
        oc = o - jnp.mean(o, axis=-1, keepdims=True)
        o = oc * lax.rsqrt(jnp.mean(oc * oc, axis=-1, keepdims=True) + RMS_EPS) * nw
        g = g_ref[sl, :].astype(F32)
        o_ref[sl, :] = (o * (g * _sigmoid(g))).astype(o_ref.dtype)


def _ret_core(proj, cos, sin, norm_w, *, batch, seq_len, tc, name):
    M = proj.shape[0]
    vd = norm_w.shape[0]
    H = RET_HEADS
    dv = vd // H
    dk = cos.shape[1] * 2
    kd = H * dk
    nt = seq_len // tc
    row = lambda b, h, n: b * nt + n
    return pl.pallas_call(
        functools.partial(_ret_kernel, n_chunks=tc // CHUNK, k_scale=dk ** -0.5),
        grid=(batch, H, nt),
        in_specs=[
            pl.BlockSpec((tc, dk), lambda b, h, n: (row(b, h, n), h)),
            pl.BlockSpec((tc, dk), lambda b, h, n: (row(b, h, n), H + h)),
            pl.BlockSpec((tc, dv), lambda b, h, n: (row(b, h, n), 2 * kd // dv + h)),
            pl.BlockSpec((tc, dv), lambda b, h, n: (row(b, h, n), (2 * kd + vd) // dv + h)),
            pl.BlockSpec((tc, dk // 2), lambda b, h, n: (n, 0)),
            pl.BlockSpec((tc, dk // 2), lambda b, h, n: (n, 0)),
            pl.BlockSpec((1, dv), lambda b, h, n: (0, h)),
        ],
        out_specs=pl.BlockSpec((tc, dv), lambda b, h, n: (row(b, h, n), h)),
        out_shape=jax.ShapeDtypeStruct((M, vd), BF16),
        scratch_shapes=[pltpu.VMEM((dv, dk), F32)],
        compiler_params=_params(("arbitrary", "arbitrary", "arbitrary")),
        name=name,
    )(proj, proj, proj, proj, cos, sin, norm_w.reshape(1, vd))


def _lru_kernel(xb_ref, yb_ref, cw_ref, cb_ref, wga_ref, bga_ref, wgx_ref, bgx_ref, lam_ref, o_ref,
                hcar_ref, ccar_ref, a_ref, u_ref, hs_ref, *, n_blocks):
    tt, width = xb_ref.shape
    kw = cw_ref.shape[0]

    @pl.when(pl.program_id(1) == 0)
    def _():
        hcar_ref[...] = jnp.zeros_like(hcar_ref)
        ccar_ref[...] = jnp.zeros_like(ccar_ref)

    bw = width // n_blocks
    for nb in range(n_blocks):
        cs = pl.ds(nb * bw, bw)
        xb = xb_ref[:, cs].astype(F32)
        ext = jnp.concatenate([ccar_ref[:, cs], xb], axis=0)
        ccar_ref[:, cs] = xb[tt - SUBLANES:, :]
        cw = cw_ref[:, cs]
        xc = cw[kw - 1:kw, :] * xb + cb_ref[:, cs]
        for d in range(1, kw):
            xc = xc + cw[kw - 1 - d:kw - d, :] * ext[SUBLANES - d:SUBLANES - d + tt, :]
        xcb = xc.astype(BF16)
        rg = _sigmoid(_dot(xcb, wga_ref[nb]) + bga_ref[:, cs])
        ig = _sigmoid(_dot(xcb, wgx_ref[nb]) + bgx_ref[:, cs])
        log_a = -LRU_C * rg * _softplus(-lam_ref[:, cs])
        a = jnp.exp(log_a)
        a_ref[:, cs] = a
        u_ref[:, cs] = xc * ig * jnp.sqrt(-jnp.tanh(log_a) * (a * a + 1.0))

    def step(t, h):
        h = a_ref[pl.ds(t, 1), :] * h + u_ref[pl.ds(t, 1), :]
        hs_ref[pl.ds(t, 1), :] = h
        return h

    hcar_ref[0:1, :] = lax.fori_loop(0, tt, step, hcar_ref[0:1, :], unroll=8)

    for nb in range(n_blocks):
        cs = pl.ds(nb * bw, bw)
        o_ref[:, cs] = (hs_ref[:, cs] * _gelu_tanh(yb_ref[:, cs].astype(F32))).astype(o_ref.dtype)


def _lru_core(proj, conv_w, conv_b, w_ga, b_ga, w_gx, b_gx, lam, *, batch, seq_len, tt, name):
    M = proj.shape[0]
    width = conv_w.shape[1]
    nt = seq_len // tt
    nb = w_ga.shape[0]
    vec = lambda: pl.BlockSpec((1, width), lambda b, n: (0, 0))
    wblk = lambda: pl.BlockSpec(w_ga.shape, lambda b, n: (0, 0, 0))
    return pl.pallas_call(
        functools.partial(_lru_kernel, n_blocks=nb),
        grid=(batch, nt),
        in_specs=[
            pl.BlockSpec((tt, width), lambda b, n: (b * nt + n, 0)),
            pl.BlockSpec((tt, width), lambda b, n: (b * nt + n, 1)),
            pl.BlockSpec(conv_w.shape, lambda b, n: (0, 0)),
            vec(), wblk(), vec(), wblk(), vec(), vec(),
        ],
        out_specs=pl.BlockSpec((tt, width), lambda b, n: (b * nt + n, 0)),
        out_shape=jax.ShapeDtypeStruct((M, width), BF16),
        scratch_shapes=[
            pltpu.VMEM((SUBLANES, width), F32),
            pltpu.VMEM((SUBLANES, width), F32),
            pltpu.VMEM((tt, width), F32),
            pltpu.VMEM((tt, width), F32),
            pltpu.VMEM((tt, width), F32),
        ],
        compiler_params=_params(("arbitrary", "arbitrary")),
        name=name,
    )(proj, proj, conv_w, conv_b.reshape(1, width), w_ga, b_ga.reshape(1, width),
      w_gx, b_gx.reshape(1, width), lam.reshape(1, width))


def kernel(x, norm_mix_w, norm_ffn_w, norm_out_w, gla_w_in, gla_w_gk, gla_b_gk, gla_norm_w, gla_w_out, lru_w_in, lru_conv_w, lru_conv_b, lru_w_ga, lru_b_ga, lru_w_gx, lru_b_gx, lru_lambda, lru_w_out, ret_w_in, ret_norm_w, ret_w_out, ffn_w_up, ffn_conv_w, ffn_conv_b, ffn_w_down):
    B, S, D = x.shape
    M = B * S
    depth = norm_mix_w.shape[0]
    xf = x.reshape(M, D)

    gla_kd = gla_w_gk.shape[2]
    gla_vd = gla_norm_w.shape[1]
    gla_main = 2 * gla_kd + 2 * gla_vd
    rank = gla_w_gk.shape[1]
    ret_dk = ret_w_in.shape[2] // (6 * RET_HEADS)
    cos = sin = None
    tm_p = min(1024, M)
    tm_f = min(512, S)
    tc = min(256, S)

    for i in range(depth):
        kind, l = i % N_MIXERS, i // N_MIXERS
        if kind == 0:
            w_in = gla_w_in[l].astype(BF16)
            w_z = jnp.pad(w_in[:, gla_main:], ((0, 0), (0, LANES - rank)))
            w_gk = jnp.pad(gla_w_gk[l].astype(BF16), ((0, LANES - rank), (0, 0)))
            proj, z = _norm_proj(xf, norm_mix_w[i], w_in[:, :gla_main], w_z,
                                 tm=tm_p,tn=1024, name=f"gla_in_{i}")
            y = _gla_core(proj, z, w_gk, gla_b_gk[l], gla_norm_w[l],
                          batch=B, seq_len=S, tc=tc,name=f"gla_core_{i}")
            xf = _proj_res(y, gla_w_out[l].astype(BF16), xf, tm=tm_p,tn=1024, name=f"gla_out_{i}")
        elif kind == 1:
            proj = _norm_proj(xf, norm_mix_w[i], lru_w_in[l].astype(BF16),
                              tm=tm_p,tn=1024, name=f"lru_in_{i}")
            y = _lru_core(proj, lru_conv_w[l], lru_conv_b[l], lru_w_ga[l].astype(BF16), lru_b_ga[l],
                          lru_w_gx[l].astype(BF16), lru_b_gx[l], lru_lambda[l],
                          batch=B, seq_len=S, tt=tc,name=f"lru_core_{i}")
            xf = _proj_res(y, lru_w_out[l].astype(BF16), xf, tm=tm_p,tn=1024, name=f"lru_out_{i}")
        else:
            if cos is None:
                cos, sin = _rope_tables(S, ret_dk // 2, tt=tm_f)
            proj = _norm_proj(xf, norm_mix_w[i], ret_w_in[l].astype(BF16),
                              tm=tm_p,tn=1024, name=f"ret_in_{i}")
            y = _ret_core(proj, cos, sin, ret_norm_w[l],
                          batch=B, seq_len=S, tc=tc,name=f"ret_core_{i}")
            xf = _proj_res(y, ret_w_out[l].astype(BF16), xf, tm=tm_p,tn=512, name=f"ret_out_{i}")
        xf = _ffn(xf, norm_ffn_w[i], ffn_w_up[i].astype(BF16), ffn_conv_w[i], ffn_conv_b[i],
                  ffn_w_down[i].astype(BF16), norm_out_w if i == depth - 1 else None,
                  seq_len=S, tm=tm_f, tf=512, name=f"ffn_{i}")
    return xf.reshape(B, S, D)
```

```python
import functools
import math

import jax
import jax.numpy as jnp
from jax import lax
from jax.experimental import pallas as pl
from jax.experimental.pallas import tpu as pltpu

F32 = jnp.float32
BF16 = jnp.bfloat16

CHUNK = 64
RMS_EPS = 1e-6
N_MIXERS = 3
GLA_HEADS = 4
GLA_GATE_TAU = 16.0
LRU_BLOCK_W = 256
LRU_C = 8.0
RET_HEADS = 8
ROPE_BASE = 10000.0

LANES = 128
SUBLANES = 8
MXU_TILE = 256
ROW_BLK = 64
VMEM_LIMIT = 56 * 1024 * 1024


def _params(sem):
    return pltpu.CompilerParams(dimension_semantics=sem, vmem_limit_bytes=VMEM_LIMIT)


def _sigmoid(x):
    return 1.0 / (1.0 + jnp.exp(-x))


def _gelu_tanh(x):
    c = math.sqrt(2.0 / math.pi)
    return x * (0.5 * (1.0 + jnp.tanh(c * (x + 0.044715 * (x * x * x)))))


def _softplus(x):
    return jnp.maximum(x, 0.0) + jnp.log1p(jnp.exp(-jnp.abs(x)))


def _dot(a, b):
    return jnp.dot(a, b, preferred_element_type=F32)


def _dot_nt(a, b):
    return lax.dot_general(a, b, (((1,), (1,)), ((), ())), preferred_element_type=F32)


def _dot_tn(a, b):
    return lax.dot_general(a, b, (((0,), (0,)), ((), ())), preferred_element_type=F32)


def _norm_proj_kernel(x_ref, nw_ref, w_ref, *rest, has_extra):
    if has_extra:
        we_ref, o_ref, oe_ref, h_ref = rest
    else:
        o_ref, h_ref = rest

    @pl.when(pl.program_id(1) == 0)
    def _():
        x = x_ref[...]
        ms = jnp.mean(x * x, axis=-1, keepdims=True)
        h_ref[...] = (x * lax.rsqrt(ms + RMS_EPS) * nw_ref[...]).astype(BF16)
        if has_extra:
            oe_ref[...] = _dot(h_ref[...], we_ref[...]).astype(oe_ref.dtype)

    o_ref[...] = _dot(h_ref[...], w_ref[...]).astype(o_ref.dtype)


def _norm_proj(x, nw, w, w_extra=None, *, tm, tn, name):
    M, D = x.shape
    N = w.shape[1]
    has_extra = w_extra is not None
    in_specs = [
        pl.BlockSpec((tm, D), lambda i, j: (i, 0)),
        pl.BlockSpec((1, D), lambda i, j: (0, 0)),
        pl.BlockSpec((D, tn), lambda i, j: (0, j)),
    ]
    args = [x, nw.reshape(1, D), w]
    out_shape = [jax.ShapeDtypeStruct((M, N), BF16)]
    out_specs = [pl.BlockSpec((tm, tn), lambda i, j: (i, j))]
    if has_extra:
        ne = w_extra.shape[1]
        in_specs.append(pl.BlockSpec((D, ne), lambda i, j: (0, 0)))
        args.append(w_extra)
        out_shape.append(jax.ShapeDtypeStruct((M, ne), BF16))
        out_specs.append(pl.BlockSpec((tm, ne), lambda i, j: (i, 0)))
    res = pl.pallas_call(
        functools.partial(_norm_proj_kernel, has_extra=has_extra),
        grid=(M // tm, N // tn),
        in_specs=in_specs,
        out_specs=out_specs,
        out_shape=out_shape,
        scratch_shapes=[pltpu.VMEM((tm, D), BF16)],
        compiler_params=_params(("parallel", "arbitrary")),
        name=name,
    )(*args)
    return res if has_extra else res[0]


def _proj_res_kernel(y_ref, w_ref, x_ref, o_ref):
    o_ref[...] = x_ref[...] + _dot(y_ref[...], w_ref[...])


def _proj_res(y, w, x, *, tm, tn, name):
    M, K = y.shape
    N = w.shape[1]
    return pl.pallas_call(
        _proj_res_kernel,
        grid=(M // tm, N // tn),
        in_specs=[
            pl.BlockSpec((tm, K), lambda i, j: (i, 0)),
            pl.BlockSpec((K, tn), lambda i, j: (0, j)),
            pl.BlockSpec((tm, tn), lambda i, j: (i, j)),
        ],
        out_specs=pl.BlockSpec((tm, tn), lambda i, j: (i, j)),
        out_shape=jax.ShapeDtypeStruct((M, N), F32),
        compiler_params=_params(("parallel", "arbitrary")),
        name=name,
    )(y, w, x)


def _interleave(*lists):
    tagged = []
    for li, steps in enumerate(lists):
        tagged += [((k + 0.5) / len(steps), li, f) for k, f in enumerate(steps)]
    tagged.sort(key=lambda t: t[:2])
    return [f for _, _, f in tagged]


def _ffn_kernel(x_ref, nw_ref, wg_ref, wv_ref, cwg_ref, cwv_ref, cbg_ref, cbv_ref, wd_ref, *rest,
                tiles_per_seq, final_norm, sub):
    if final_norm:
        fnw_ref, o_ref, h_ref, carg_ref, carv_ref, ug_ref, uv_ref, act_ref = rest
    else:
        o_ref, h_ref, carg_ref, carv_ref, ug_ref, uv_ref, act_ref = rest
    i = pl.program_id(0)
    j = pl.program_id(1)
    nj = pl.num_programs(1)
    tm, dm = x_ref.shape
    tf = wd_ref.shape[0]
    nsub = tf // sub
    pad = SUBLANES

    @pl.when(j == 0)
    def _():
        x = x_ref[...]
        ms = jnp.mean(x * x, axis=-1, keepdims=True)
        h_ref[...] = (x * lax.rsqrt(ms + RMS_EPS) * nw_ref[...]).astype(BF16)
        o_ref[...] = x

    seq_start = (i % tiles_per_seq) == 0

    def up_steps(s):
        p = s % 2
        cs = pl.ds(s * sub, sub)
        acc = {}

        def kdot(w_ref, key, k):
            def f():
                ks = pl.ds(k * MXU_TILE, MXU_TILE)
                d = _dot(h_ref[:, ks], w_ref[ks, cs])
                acc[key] = d if k == 0 else acc[key] + d
            return f

        def park(key, u_ref, car_ref):
            def f():
                u = acc[key]
                u_ref[p, 0:pad, :] = jnp.where(seq_start, 0.0, car_ref[j, :, cs])
                car_ref[j, :, cs] = u[tm - pad:, :]
                u_ref[p, pad:pad + tm, :] = u
            return f

        steps = [kdot(wg_ref, "g", k) for k in range(dm // MXU_TILE)] + [park("g", ug_ref, carg_ref)]
        steps += [kdot(wv_ref, "v", k) for k in range(dm // MXU_TILE)] + [park("v", uv_ref, carv_ref)]
        return steps

    def gate_steps(s):
        p = s % 2
        cs = pl.ds(s * sub, sub)

        def conv(u_ref, cw_ref, cb_ref, r0):
            cw = cw_ref[:, cs]
            kw = cw.shape[0]
            y = cb_ref[:, cs] + cw[kw - 1:kw, :] * u_ref[p, pad + r0:pad + r0 + ROW_BLK, :]
            for d in range(1, kw):
                y = y + cw[kw - 1 - d:kw - d, :] * u_ref[p, pad + r0 - d:pad + r0 - d + ROW_BLK, :]
            return y

        def block(r0):
            def f():
                gate = conv(ug_ref, cwg_ref, cbg_ref, r0)
                val = conv(uv_ref, cwv_ref, cbv_ref, r0)
                act_ref[p, r0:r0 + ROW_BLK, :] = (_gelu_tanh(gate) * val).astype(BF16)
            return f

        return [block(r0) for r0 in range(0, tm, ROW_BLK)]

    def down_steps(s):
        p = s % 2
        cs = pl.ds(s * sub, sub)

        def ndot(n):
            def f():
                ns = pl.ds(n * MXU_TILE, MXU_TILE)
                o_ref[:, ns] += _dot(act_ref[p], wd_ref[cs, ns])
            return f

        return [ndot(n) for n in range(dm // MXU_TILE)]

    for f in up_steps(0):
        f()
    for s in range(nsub):
        slot = _interleave(gate_steps(s),
                           up_steps(s + 1) if s + 1 < nsub else [],
                           down_steps(s - 1) if s > 0 else [])
        for f in slot:
            f()
    for f in down_steps(nsub - 1):
        f()

    if final_norm:
        @pl.when(j == nj - 1)
        def _():
            y = o_ref[...]
            ms = jnp.mean(y * y, axis=-1, keepdims=True)
            o_ref[...] = y * lax.rsqrt(ms + RMS_EPS) * fnw_ref[...]


def _ffn(x, nw, w_up, conv_w, conv_b, w_down, final_nw, *, seq_len, tm, tf, sub, name):
    M, D = x.shape
    F = w_down.shape[0]
    nj = F // tf
    final_norm = final_nw is not None
    in_specs = [
        pl.BlockSpec((tm, D), lambda i, j: (i, 0)),
        pl.BlockSpec((1, D), lambda i, j: (0, 0)),
        pl.BlockSpec((D, tf), lambda i, j: (0, j)),
        pl.BlockSpec((D, tf), lambda i, j: (0, nj + j)),
        pl.BlockSpec((conv_w.shape[0], tf), lambda i, j: (0, j)),
        pl.BlockSpec((conv_w.shape[0], tf), lambda i, j: (0, nj + j)),
        pl.BlockSpec((1, tf), lambda i, j: (0, j)),
        pl.BlockSpec((1, tf), lambda i, j: (0, nj + j)),
        pl.BlockSpec((tf, D), lambda i, j: (j, 0)),
    ]
    cb = conv_b.reshape(1, 2 * F)
    args = [x, nw.reshape(1, D), w_up, w_up, conv_w, conv_w, cb, cb, w_down]
    if final_norm:
        in_specs.append(pl.BlockSpec((1, D), lambda i, j: (0, 0)))
        args.append(final_nw.reshape(1, D))
    return pl.pallas_call(
        functools.partial(_ffn_kernel, tiles_per_seq=seq_len // tm, final_norm=final_norm, sub=sub),
        grid=(M // tm, nj),
        in_specs=in_specs,
        out_specs=pl.BlockSpec((tm, D), lambda i, j: (i, 0)),
        out_shape=jax.ShapeDtypeStruct((M, D), F32),
        scratch_shapes=[
            pltpu.VMEM((tm, D), BF16),
            pltpu.VMEM((nj, SUBLANES, tf), F32),
            pltpu.VMEM((nj, SUBLANES, tf), F32),
            pltpu.VMEM((2, SUBLANES + tm, sub), F32),
            pltpu.VMEM((2, SUBLANES + tm, sub), F32),
            pltpu.VMEM((2, tm, sub), BF16),
        ],
        compiler_params=_params(("arbitrary", "arbitrary")),
        name=name,
    )(*args)


def _tile_masks(tc):
    r = lax.broadcasted_iota(jnp.int32, (tc, tc), 0)
    c = lax.broadcasted_iota(jnp.int32, (tc, tc), 1)
    shift = CHUNK.bit_length() - 1
    same = lax.shift_right_logical(r, shift) == lax.shift_right_logical(c, shift)
    return r, c, same


def _gla_kernel(q_ref, k_ref, v_ref, g_ref, z_ref, wgk_ref, bgk_ref, nw_ref, o_ref, st_ref, *,
                n_chunks, q_scale):
    @pl.when(pl.program_id(2) == 0)
    def _():
        st_ref[...] = jnp.zeros_like(st_ref)

    tc, dk = q_ref.shape
    r, c, same = _tile_masks(tc)
    lower = same & (r >= c)
    upper = same & (r < c)
    tri = lower.astype(BF16)

    pre = _dot(z_ref[...], wgk_ref[...]) + bgk_ref[...]
    log_a = -_softplus(-pre) / GLA_GATE_TAU
    la_hi = log_a.astype(BF16)
    la_lo = (log_a - la_hi.astype(F32)).astype(BF16)
    b = _dot(tri, la_hi) + _dot(tri, la_lo)
    b_last = [b[(ci + 1) * CHUNK - 1:(ci + 1) * CHUNK, :] for ci in range(n_chunks)]
    b_last_rows = jnp.concatenate([jnp.broadcast_to(bl, (CHUNK, dk)) for bl in b_last], axis=0)
    eb = jnp.exp(b)
    enb = jnp.exp(-b)
    q = q_ref[...].astype(F32) * q_scale
    k = k_ref[...].astype(F32)
    v = v_ref[...]
    q_dec = (q * eb).astype(BF16)
    k_grow = (k * enb).astype(BF16)
    q_grow = (q * enb).astype(BF16)
    k_dec = (k * eb).astype(BF16)
    k_in = (k * jnp.exp(b_last_rows - b)).astype(BF16)
    scores = jnp.where(lower, _dot_nt(q_dec, k_grow), jnp.where(upper, _dot_nt(q_grow, k_dec), 0.0))
    o_intra = _dot(scores.astype(BF16), v)

    chunk_rows = [slice(ci * CHUNK, (ci + 1) * CHUNK) for ci in range(n_chunks)]
    incs = [_dot_tn(v[rows, :], k_in[rows, :]) for rows in chunk_rows]
    states = [st_ref[...]]
    for ci in range(n_chunks):
        states.append(jnp.exp(b_last[ci]) * states[ci] + incs[ci])
    st_ref[...] = states[n_chunks]
    o = jnp.concatenate(
        [o_intra[rows, :] + _dot_nt(q_dec[rows, :], states[ci].astype(BF16))
         for ci, rows in enumerate(chunk_rows)], axis=0)
    o = o * lax.rsqrt(jnp.mean(o * o, axis=-1, keepdims=True) + RMS_EPS) * nw_ref[...]
    g = g_ref[...].astype(F32)
    o_ref[...] = (o * (g * _sigmoid(g))).astype(o_ref.dtype)


def _gla_core(proj, z, wgk, bgk, norm_w, *, batch, seq_len, tc, name):
    M = proj.shape[0]
    kd = wgk.shape[1]
    vd = norm_w.shape[0]
    H = GLA_HEADS
    dk, dv = kd // H, vd // H
    nt = seq_len // tc
    row = lambda b, h, n: b * nt + n
    return pl.pallas_call(
        functools.partial(_gla_kernel, n_chunks=tc // CHUNK, q_scale=dk ** -0.5),
        grid=(batch, H, nt),
        in_specs=[
            pl.BlockSpec((tc, dk), lambda b, h, n: (row(b, h, n), h)),
            pl.BlockSpec((tc, dk), lambda b, h, n: (row(b, h, n), kd // dk + h)),
            pl.BlockSpec((tc, dv), lambda b, h, n: (row(b, h, n), 2 * kd // dv + h)),
            pl.BlockSpec((tc, dv), lambda b, h, n: (row(b, h, n), (2 * kd + vd) // dv + h)),
            pl.BlockSpec((tc, z.shape[1]), lambda b, h, n: (row(b, h, n), 0)),
            pl.BlockSpec((wgk.shape[0], dk), lambda b, h, n: (0, h)),
            pl.BlockSpec((1, dk), lambda b, h, n: (0, h)),
            pl.BlockSpec((1, dv), lambda b, h, n: (0, h)),
        ],
        out_specs=pl.BlockSpec((tc, dv), lambda b, h, n: (row(b, h, n), h)),
        out_shape=jax.ShapeDtypeStruct((M, vd), BF16),
        scratch_shapes=[pltpu.VMEM((dv, dk), F32)],
        compiler_params=_params(("arbitrary", "arbitrary", "arbitrary")),
        name=name,
    )(proj, proj, proj, proj, z, wgk, bgk.reshape(1, kd), norm_w.reshape(1, vd))


def _rope_kernel(cos_ref, sin_ref):
    tt, half = cos_ref.shape
    pos = (pl.program_id(0) * tt + lax.broadcasted_iota(jnp.int32, (tt, half), 0)).astype(F32)
    jj = lax.broadcasted_iota(jnp.int32, (tt, half), 1).astype(F32)
    inv = jnp.exp(-(jj / half) * math.log(ROPE_BASE))
    ang = pos * inv
    cos_ref[...] = jnp.cos(ang)
    sin_ref[...] = jnp.sin(ang)


def _rope_tables(seq_len, half, *, tt):
    return pl.pallas_call(
        _rope_kernel,
        grid=(seq_len // tt,),
        out_specs=[pl.BlockSpec((tt, half), lambda i: (i, 0))] * 2,
        out_shape=[jax.ShapeDtypeStruct((seq_len, half), F32)] * 2,
        compiler_params=_params(("parallel",)),
        name="rope_tables",
    )()


def _ret_kernel(q_ref, k_ref, v_ref, g_ref, cos_ref, sin_ref, nw_ref, o_ref,
                st_ref, dmat_ref, xi_ref, zeta_ref, *, k_scale):
    tc, dk = q_ref.shape
    half = dk // 2
    head = pl.program_id(1).astype(F32)

    def log_gamma(shape):
        return jnp.log(1.0 - jnp.exp2(-5.0 - jnp.full(shape, head, F32)))

    @pl.when(pl.program_id(2) == 0)
    def _():
        st_ref[...] = jnp.zeros_like(st_ref)
        r, c, _ = _tile_masks(tc)
        shift = CHUNK.bit_length() - 1
        reads = lax.shift_right_logical(c, shift) <= lax.shift_right_logical(r, shift)
        dist = jnp.abs(r - c).astype(F32)
        dmat_ref[...] = jnp.where(reads, jnp.exp(log_gamma((tc, tc)) * dist), 0.0)
        pos = lax.broadcasted_iota(jnp.int32, (tc, dk), 0).astype(F32)
        lg = log_gamma((tc, dk))
        xi_ref[...] = jnp.exp(lg * (pos + 1.0))
        zeta_ref[...] = jnp.exp(lg * (tc - 1.0 - pos))

    decay = jnp.exp(log_gamma((1, dk)) * float(tc))

    def rope(x):
        x1, x2 = x[:, :half], x[:, half:]
        cos, sin = cos_ref[...], sin_ref[...]
        return jnp.concatenate([x1 * cos - x2 * sin, x2 * cos + x1 * sin], axis=-1)

    q = rope(q_ref[...].astype(F32))
    k = rope(k_ref[...].astype(F32)) * k_scale
    v = v_ref[...]
    scores = _dot_nt(q.astype(BF16), k.astype(BF16)) * dmat_ref[...]
    st = st_ref[...]
    o = _dot(scores.astype(BF16), v) + _dot_nt((q * xi_ref[...]).astype(BF16), st.astype(BF16))
    st_ref[...] = decay * st + _dot_tn(v, (k * zeta_ref[...]).astype(BF16))
    oc = o - jnp.mean(o, axis=-1, keepdims=True)
    o = oc * lax.rsqrt(jnp.mean(oc * oc, axis=-1, keepdims=True) + RMS_EPS) * nw_ref[...]
    g = g_ref[...].astype(F32)
    o_ref[...] = (o * (g * _sigmoid(g))).astype(o_ref.dtype)


def _ret_core(proj, cos, sin, norm_w, *, batch, seq_len, tc, name):
    M = proj.shape[0]
    vd = norm_w.shape[0]
    H = RET_HEADS
    dv = vd // H
    dk = cos.shape[1] * 2
    kd = H * dk
    nt = seq_len // tc
    row = lambda b, h, n: b * nt + n
    return pl.pallas_call(
        functools.partial(_ret_kernel, k_scale=dk ** -0.5),
        grid=(batch, H, nt),
        in_specs=[
            pl.BlockSpec((tc, dk), lambda b, h, n: (row(b, h, n), h)),
            pl.BlockSpec((tc, dk), lambda b, h, n: (row(b, h, n), H + h)),
            pl.BlockSpec((tc, dv), lambda b, h, n: (row(b, h, n), 2 * kd // dv + h)),
            pl.BlockSpec((tc, dv), lambda b, h, n: (row(b, h, n), (2 * kd + vd) // dv + h)),
            pl.BlockSpec((tc, dk // 2), lambda b, h, n: (n, 0)),
            pl.BlockSpec((tc, dk // 2), lambda b, h, n: (n, 0)),
            pl.BlockSpec((1, dv), lambda b, h, n: (0, h)),
        ],
        out_specs=pl.BlockSpec((tc, dv), lambda b, h, n: (row(b, h, n), h)),
        out_shape=jax.ShapeDtypeStruct((M, vd), BF16),
        scratch_shapes=[
            pltpu.VMEM((dv, dk), F32),
            pltpu.VMEM((tc, tc), F32),
            pltpu.VMEM((tc, dk), F32),
            pltpu.VMEM((tc, dk), F32),
        ],
        compiler_params=_params(("arbitrary", "arbitrary", "arbitrary")),
        name=name,
    )(proj, proj, proj, proj, cos, sin, norm_w.reshape(1, vd))


def _lru_kernel(xb_ref, yb_ref, cw_ref, cb_ref, wga_ref, bga_ref, wgx_ref, bgx_ref, lam_ref, o_ref,
                hcar_ref, ccar_ref, a_ref, u_ref, hs_ref, *, n_blocks):
    tt, width = xb_ref.shape
    kw = cw_ref.shape[0]

    @pl.when(pl.program_id(1) == 0)
    def _():
        hcar_ref[...] = jnp.zeros_like(hcar_ref)
        ccar_ref[...] = jnp.zeros_like(ccar_ref)

    bw = width // n_blocks
    for nb in range(n_blocks):
        cs = pl.ds(nb * bw, bw)
        xb = xb_ref[:, cs].astype(F32)
        ext = jnp.concatenate([ccar_ref[:, cs], xb], axis=0)
        ccar_ref[:, cs] = xb[tt - SUBLANES:, :]
        cw = cw_ref[:, cs]
        xc = cw[kw - 1:kw, :] * xb + cb_ref[:, cs]
        for d in range(1, kw):
            xc = xc + cw[kw - 1 - d:kw - d, :] * ext[SUBLANES - d:SUBLANES - d + tt, :]
        xcb = xc.astype(BF16)
        rg = _sigmoid(_dot(xcb, wga_ref[nb]) + bga_ref[:, cs])
        ig = _sigmoid(_dot(xcb, wgx_ref[nb]) + bgx_ref[:, cs])
        log_a = -LRU_C * rg * _softplus(-lam_ref[:, cs])
        a = jnp.exp(log_a)
        a_ref[:, cs] = a
        u_ref[:, cs] = xc * ig * jnp.sqrt(-jnp.tanh(log_a) * (a * a + 1.0))

    def step(t, h):
        h = a_ref[pl.ds(t, 1), :] * h + u_ref[pl.ds(t, 1), :]
        hs_ref[pl.ds(t, 1), :] = h
        return h

    hcar_ref[0:1, :] = lax.fori_loop(0, tt, step, hcar_ref[0:1, :], unroll=8)

    for nb in range(n_blocks):
        cs = pl.ds(nb * bw, bw)
        o_ref[:, cs] = (hs_ref[:, cs] * _gelu_tanh(yb_ref[:, cs].astype(F32))).astype(o_ref.dtype)


def _lru_core(proj, conv_w, conv_b, w_ga, b_ga, w_gx, b_gx, lam, *, batch, seq_len, tt, name):
    M = proj.shape[0]
    width = conv_w.shape[1]
    nt = seq_len // tt
    nb = w_ga.shape[0]
    vec = lambda: pl.BlockSpec((1, width), lambda b, n: (0, 0))
    wblk = lambda: pl.BlockSpec(w_ga.shape, lambda b, n: (0, 0, 0))
    return pl.pallas_call(
        functools.partial(_lru_kernel, n_blocks=nb),
        grid=(batch, nt),
        in_specs=[
            pl.BlockSpec((tt, width), lambda b, n: (b * nt + n, 0)),
            pl.BlockSpec((tt, width), lambda b, n: (b * nt + n, 1)),
            pl.BlockSpec(conv_w.shape, lambda b, n: (0, 0)),
            vec(), wblk(), vec(), wblk(), vec(), vec(),
        ],
        out_specs=pl.BlockSpec((tt, width), lambda b, n: (b * nt + n, 0)),
        out_shape=jax.ShapeDtypeStruct((M, width), BF16),
        scratch_shapes=[
            pltpu.VMEM((SUBLANES, width), F32),
            pltpu.VMEM((SUBLANES, width), F32),
            pltpu.VMEM((tt, width), F32),
            pltpu.VMEM((tt, width), F32),
            pltpu.VMEM((tt, width), F32),
        ],
        compiler_params=_params(("arbitrary", "arbitrary")),
        name=name,
    )(proj, proj, conv_w, conv_b.reshape(1, width), w_ga, b_ga.reshape(1, width),
      w_gx, b_gx.reshape(1, width), lam.reshape(1, width))


def kernel(x, norm_mix_w, norm_ffn_w, norm_out_w, gla_w_in, gla_w_gk, gla_b_gk, gla_norm_w, gla_w_out, lru_w_in, lru_conv_w, lru_conv_b, lru_w_ga, lru_b_ga, lru_w_gx, lru_b_gx, lru_lambda, lru_w_out, ret_w_in, ret_norm_w, ret_w_out, ffn_w_up, ffn_conv_w, ffn_conv_b, ffn_w_down):
    B, S, D = x.shape
    M = B * S
    depth = norm_mix_w.shape[0]
    xf = x.reshape(M, D)

    gla_kd = gla_w_gk.shape[2]
    gla_vd = gla_norm_w.shape[1]
    gla_main = 2 * gla_kd + 2 * gla_vd
    rank = gla_w_gk.shape[1]
    ret_dk = ret_w_in.shape[2] // (6 * RET_HEADS)
    cos = sin = None
    tm_p = min(1024, M)
    tm_f = min(512, S)
    tc = min(256, S)

    for i in range(depth):
        kind, l = i % N_MIXERS, i // N_MIXERS
        if kind == 0:
            w_in = gla_w_in[l].astype(BF16)
            w_z = jnp.pad(w_in[:, gla_main:], ((0, 0), (0, LANES - rank)))
            w_gk = jnp.pad(gla_w_gk[l].astype(BF16), ((0, LANES - rank), (0, 0)))
            proj, z = _norm_proj(xf, norm_mix_w[i], w_in[:, :gla_main], w_z,
                                 tm=tm_p,tn=1024, name=f"gla_in_{i}")
            y = _gla_core(proj, z, w_gk, gla_b_gk[l], gla_norm_w[l],
                          batch=B, seq_len=S, tc=tc,name=f"gla_core_{i}")
            xf = _proj_res(y, gla_w_out[l].astype(BF16), xf, tm=tm_p,tn=1024, name=f"gla_out_{i}")
        elif kind == 1:
            proj = _norm_proj(xf, norm_mix_w[i], lru_w_in[l].astype(BF16),
                              tm=tm_p,tn=1024, name=f"lru_in_{i}")
            y = _lru_core(proj, lru_conv_w[l], lru_conv_b[l], lru_w_ga[l].astype(BF16), lru_b_ga[l],
                          lru_w_gx[l].astype(BF16), lru_b_gx[l], lru_lambda[l],
                          batch=B, seq_len=S, tt=tc,name=f"lru_core_{i}")
            xf = _proj_res(y, lru_w_out[l].astype(BF16), xf, tm=tm_p,tn=1024, name=f"lru_out_{i}")
        else:
            if cos is None:
                cos, sin = _rope_tables(S, ret_dk // 2, tt=tm_f)
            proj = _norm_proj(xf, norm_mix_w[i], ret_w_in[l].astype(BF16),
                              tm=tm_p,tn=1024, name=f"ret_in_{i}")
            y = _ret_core(proj, cos, sin, ret_norm_w[l],
                          batch=B, seq_len=S, tc=tc,name=f"ret_core_{i}")
            xf = _proj_res(y, ret_w_out[l].astype(BF16), xf, tm=tm_p,tn=512, name=f"ret_out_{i}")
        xf = _ffn(xf, norm_ffn_w[i], ffn_w_up[i].astype(BF16), ffn_conv_w[i], ffn_conv_b[i],
                  ffn_w_down[i].astype(BF16), norm_out_w if i == depth - 1 else None,
                  seq_len=S, tm=tm_f, tf=1024, sub=256, name=f"ffn_{i}")
    return xf.reshape(B, S, D)
```

```python
import functools
import math

import jax
import jax.numpy as jnp
from jax import lax
from jax.experimental import pallas as pl
from jax.experimental.pallas import tpu as pltpu

F32 = jnp.float32
BF16 = jnp.bfloat16

CHUNK = 64
RMS_EPS = 1e-6
N_MIXERS = 3
GLA_HEADS = 4
GLA_GATE_TAU = 16.0
LRU_BLOCK_W = 256
LRU_C = 8.0
RET_HEADS = 8
ROPE_BASE = 10000.0

LANES = 128
SUBLANES = 8
MXU_TILE = 256
ROW_BLK = 64
VMEM_LIMIT = 56 * 1024 * 1024


def _params(sem):
    return pltpu.CompilerParams(dimension_semantics=sem, vmem_limit_bytes=VMEM_LIMIT)


def _sigmoid(x):
    return 0.5 * jnp.tanh(0.5 * x) + 0.5


def _gelu_tanh(x):
    c = math.sqrt(2.0 / math.pi)
    return x * (0.5 * (1.0 + jnp.tanh(c * (x + 0.044715 * (x * x * x)))))


def _softplus(x):
    return jnp.maximum(x, 0.0) + jnp.log1p(jnp.exp(-jnp.abs(x)))


def _dot(a, b):
    return jnp.dot(a, b, preferred_element_type=F32)


def _dot_nt(a, b):
    return lax.dot_general(a, b, (((1,), (1,)), ((), ())), preferred_element_type=F32)


def _dot_tn(a, b):
    return lax.dot_general(a, b, (((0,), (0,)), ((), ())), preferred_element_type=F32)


def _norm_proj_kernel(x_ref, nw_ref, w_ref, *rest, has_extra):
    if has_extra:
        we_ref, o_ref, oe_ref, h_ref = rest
    else:
        o_ref, h_ref = rest

    @pl.when(pl.program_id(1) == 0)
    def _():
        x = x_ref[...]
        ms = jnp.mean(x * x, axis=-1, keepdims=True)
        h_ref[...] = (x * lax.rsqrt(ms + RMS_EPS) * nw_ref[...]).astype(BF16)
        if has_extra:
            oe_ref[...] = _dot(h_ref[...], we_ref[...]).astype(oe_ref.dtype)

    o_ref[...] = _dot(h_ref[...], w_ref[...]).astype(o_ref.dtype)


def _norm_proj(x, nw, w, layer, w_extra=None, *, n_out=None, tm, tn, name):
    M, D = x.shape
    N = w.shape[2] if n_out is None else n_out
    has_extra = w_extra is not None
    in_specs = [
        pl.BlockSpec((tm, D), lambda i, j: (i, 0)),
        pl.BlockSpec((1, D), lambda i, j: (0, 0)),
        pl.BlockSpec((None, D, tn), lambda i, j: (layer, 0, j)),
    ]
    args = [x, nw.reshape(1, D), w]
    out_shape = [jax.ShapeDtypeStruct((M, N), BF16)]
    out_specs = [pl.BlockSpec((tm, tn), lambda i, j: (i, j))]
    if has_extra:
        ne = w_extra.shape[1]
        in_specs.append(pl.BlockSpec((D, ne), lambda i, j: (0, 0)))
        args.append(w_extra)
        out_shape.append(jax.ShapeDtypeStruct((M, ne), BF16))
        out_specs.append(pl.BlockSpec((tm, ne), lambda i, j: (i, 0)))
    res = pl.pallas_call(
        functools.partial(_norm_proj_kernel, has_extra=has_extra),
        grid=(M // tm, N // tn),
        in_specs=in_specs,
        out_specs=out_specs,
        out_shape=out_shape,
        scratch_shapes=[pltpu.VMEM((tm, D), BF16)],
        compiler_params=_params(("parallel", "arbitrary")),
        name=name,
    )(*args)
    return res if has_extra else res[0]


def _proj_res_kernel(y_ref, w_ref, x_ref, o_ref):
    o_ref[...] = x_ref[...] + _dot(y_ref[...], w_ref[...])


def _proj_res(y, w, layer, x, *, tm, tn, name):
    M, K = y.shape
    N = w.shape[2]
    return pl.pallas_call(
        _proj_res_kernel,
        grid=(M // tm, N // tn),
        in_specs=[
            pl.BlockSpec((tm, K), lambda i, j: (i, 0)),
            pl.BlockSpec((None, K, tn), lambda i, j: (layer, 0, j)),
            pl.BlockSpec((tm, tn), lambda i, j: (i, j)),
        ],
        out_specs=pl.BlockSpec((tm, tn), lambda i, j: (i, j)),
        out_shape=jax.ShapeDtypeStruct((M, N), F32),
        compiler_params=_params(("parallel", "arbitrary")),
        name=name,
    )(y, w, x)


def _interleave(*lists):
    tagged = []
    for li, steps in enumerate(lists):
        tagged += [((k + 0.5) / len(steps), li, f) for k, f in enumerate(steps)]
    tagged.sort(key=lambda t: t[:2])
    return [f for _, _, f in tagged]


def _ffn_kernel(x_ref, nw_ref, wg_ref, wv_ref, cwg_ref, cwv_ref, cbg_ref, cbv_ref, wd_ref, *rest,
                tiles_per_seq, final_norm, sub):
    if final_norm:
        fnw_ref, o_ref, h_ref, carg_ref, carv_ref, ug_ref, uv_ref, act_ref = rest
    else:
        o_ref, h_ref, carg_ref, carv_ref, ug_ref, uv_ref, act_ref = rest
    i = pl.program_id(0)
    j = pl.program_id(1)
    nj = pl.num_programs(1)
    tm, dm = x_ref.shape
    tf = wd_ref.shape[0]
    nsub = tf // sub
    pad = SUBLANES

    @pl.when(j == 0)
    def _():
        x = x_ref[...]
        ms = jnp.mean(x * x, axis=-1, keepdims=True)
        h_ref[...] = (x * lax.rsqrt(ms + RMS_EPS) * nw_ref[...]).astype(BF16)
        o_ref[...] = x

    seq_start = (i % tiles_per_seq) == 0

    def up_steps(s):
        p = s % 2
        cs = pl.ds(s * sub, sub)
        acc = {}

        def kdot(w_ref, key, k):
            def f():
                ks = pl.ds(k * MXU_TILE, MXU_TILE)
                d = _dot(h_ref[:, ks], w_ref[ks, cs])
                acc[key] = d if k == 0 else acc[key] + d
            return f

        def park(key, u_ref, car_ref):
            def f():
                u = acc[key]
                u_ref[p, 0:pad, :] = jnp.where(seq_start, 0.0, car_ref[j, :, cs])
                car_ref[j, :, cs] = u[tm - pad:, :]
                u_ref[p, pad:pad + tm, :] = u
            return f

        steps = [kdot(wg_ref, "g", k) for k in range(dm // MXU_TILE)] + [park("g", ug_ref, carg_ref)]
        steps += [kdot(wv_ref, "v", k) for k in range(dm // MXU_TILE)] + [park("v", uv_ref, carv_ref)]
        return steps

    def gate_steps(s):
        p = s % 2
        cs = pl.ds(s * sub, sub)

        def conv(u_ref, cw_ref, cb_ref, r0):
            cw = cw_ref[:, cs]
            kw = cw.shape[0]
            y = cb_ref[:, cs] + cw[kw - 1:kw, :] * u_ref[p, pad + r0:pad + r0 + ROW_BLK, :]
            for d in range(1, kw):
                y = y + cw[kw - 1 - d:kw - d, :] * u_ref[p, pad + r0 - d:pad + r0 - d + ROW_BLK, :]
            return y

        def block(r0):
            def f():
                gate = conv(ug_ref, cwg_ref, cbg_ref, r0)
                val = conv(uv_ref, cwv_ref, cbv_ref, r0)
                act_ref[p, r0:r0 + ROW_BLK, :] = (_gelu_tanh(gate) * val).astype(BF16)
            return f

        return [block(r0) for r0 in range(0, tm, ROW_BLK)]

    def down_steps(s):
        p = s % 2
        cs = pl.ds(s * sub, sub)

        def ndot(n):
            def f():
                ns = pl.ds(n * MXU_TILE, MXU_TILE)
                o_ref[:, ns] += _dot(act_ref[p], wd_ref[cs, ns])
            return f

        return [ndot(n) for n in range(dm // MXU_TILE)]

    for f in up_steps(0):
        f()
    for s in range(nsub):
        slot = _interleave(gate_steps(s),
                           up_steps(s + 1) if s + 1 < nsub else [],
                           down_steps(s - 1) if s > 0 else [])
        for f in slot:
            f()
    for f in down_steps(nsub - 1):
        f()

    if final_norm:
        @pl.when(j == nj - 1)
        def _():
            y = o_ref[...]
            ms = jnp.mean(y * y, axis=-1, keepdims=True)
            o_ref[...] = y * lax.rsqrt(ms + RMS_EPS) * fnw_ref[...]


def _ffn(x, nw, w_up, conv_w, conv_b, w_down, layer, final_nw, *, seq_len, tm, tf, sub, name):
    M, D = x.shape
    F = w_down.shape[1]
    nj = F // tf
    final_norm = final_nw is not None
    in_specs = [
        pl.BlockSpec((tm, D), lambda i, j: (i, 0)),
        pl.BlockSpec((1, D), lambda i, j: (0, 0)),
        pl.BlockSpec((None, D, tf), lambda i, j: (layer, 0, j)),
        pl.BlockSpec((None, D, tf), lambda i, j: (layer, 0, nj + j)),
        pl.BlockSpec((conv_w.shape[0], tf), lambda i, j: (0, j)),
        pl.BlockSpec((conv_w.shape[0], tf), lambda i, j: (0, nj + j)),
        pl.BlockSpec((1, tf), lambda i, j: (0, j)),
        pl.BlockSpec((1, tf), lambda i, j: (0, nj + j)),
        pl.BlockSpec((None, tf, D), lambda i, j: (layer, j, 0)),
    ]
    cb = conv_b.reshape(1, 2 * F)
    args = [x, nw.reshape(1, D), w_up, w_up, conv_w, conv_w, cb, cb, w_down]
    if final_norm:
        in_specs.append(pl.BlockSpec((1, D), lambda i, j: (0, 0)))
        args.append(final_nw.reshape(1, D))
    return pl.pallas_call(
        functools.partial(_ffn_kernel, tiles_per_seq=seq_len // tm, final_norm=final_norm, sub=sub),
        grid=(M // tm, nj),
        in_specs=in_specs,
        out_specs=pl.BlockSpec((tm, D), lambda i, j: (i, 0)),
        out_shape=jax.ShapeDtypeStruct((M, D), F32),
        scratch_shapes=[
            pltpu.VMEM((tm, D), BF16),
            pltpu.VMEM((nj, SUBLANES, tf), F32),
            pltpu.VMEM((nj, SUBLANES, tf), F32),
            pltpu.VMEM((2, SUBLANES + tm, sub), F32),
            pltpu.VMEM((2, SUBLANES + tm, sub), F32),
            pltpu.VMEM((2, tm, sub), BF16),
        ],
        compiler_params=_params(("arbitrary", "arbitrary")),
        name=name,
    )(*args)


def _tile_masks(tc):
    r = lax.broadcasted_iota(jnp.int32, (tc, tc), 0)
    c = lax.broadcasted_iota(jnp.int32, (tc, tc), 1)
    shift = CHUNK.bit_length() - 1
    same = lax.shift_right_logical(r, shift) == lax.shift_right_logical(c, shift)
    return r, c, same


def _gla_kernel(q_ref, k_ref, v_ref, g_ref, z_ref, wgk_ref, bgk_ref, nw_ref, o_ref, st_ref, *,
                n_chunks, q_scale):
    @pl.when(pl.program_id(1) == 0)
    def _():
        st_ref[...] = jnp.zeros_like(st_ref)

    nb, tc, dk = q_ref.shape
    r, c, same = _tile_masks(tc)
    lower = same & (r >= c)
    upper = same & (r < c)
    tri = lower.astype(BF16)
    chunk_rows = [slice(ci * CHUNK, (ci + 1) * CHUNK) for ci in range(n_chunks)]

    def chain(bi):
        t = {}

        def gate():
            pre = _dot(z_ref[bi], wgk_ref[...]) + bgk_ref[...]
            log_a = -_softplus(-pre) / GLA_GATE_TAU
            t["hi"] = log_a.astype(BF16)
            t["lo"] = (log_a - t["hi"].astype(F32)).astype(BF16)

        def cumsum():
            b = _dot(tri, t["hi"]) + _dot(tri, t["lo"])
            t["b"] = b
            t["b_last"] = [b[(ci + 1) * CHUNK - 1:(ci + 1) * CHUNK, :] for ci in range(n_chunks)]

        def decays():
            b = t["b"]
            b_last_rows = jnp.concatenate(
                [jnp.broadcast_to(bl, (CHUNK, dk)) for bl in t["b_last"]], axis=0)
            eb = jnp.exp(b)
            enb = jnp.exp(-b)
            q = q_ref[bi].astype(F32) * q_scale
            k = k_ref[bi].astype(F32)
            t["q_dec"] = (q * eb).astype(BF16)
            t["k_grow"] = (k * enb).astype(BF16)
            t["q_grow"] = (q * enb).astype(BF16)
            t["k_dec"] = (k * eb).astype(BF16)
            t["k_in"] = (k * jnp.exp(b_last_rows - b)).astype(BF16)

        def intra():
            scores = jnp.where(lower, _dot_nt(t["q_dec"], t["k_grow"]),
                               jnp.where(upper, _dot_nt(t["q_grow"], t["k_dec"]), 0.0))
            t["o_intra"] = _dot(scores.astype(BF16), v_ref[bi])

        def state():
            v = v_ref[bi]
            incs = [_dot_tn(v[rows, :], t["k_in"][rows, :]) for rows in chunk_rows]
            states = [st_ref[bi]]
            for ci in range(n_chunks):
                states.append(jnp.exp(t["b_last"][ci]) * states[ci] + incs[ci])
            st_ref[bi] = states[n_chunks]
            t["states"] = states

        def readout():
            o = jnp.concatenate(
                [t["o_intra"][rows, :] + _dot_nt(t["q_dec"][rows, :], t["states"][ci].astype(BF16))
                 for ci, rows in enumerate(chunk_rows)], axis=0)
            o = o * lax.rsqrt(jnp.mean(o * o, axis=-1, keepdims=True) + RMS_EPS) * nw_ref[...]
            g = g_ref[bi].astype(F32)
            o_ref[bi] = (o * (g * _sigmoid(g))).astype(o_ref.dtype)

        return [gate, cumsum, decays, intra, state, readout]

    for f in _interleave(*[chain(bi) for bi in range(nb)]):
        f()


def _gla_core(proj, z, wgk, bgk, norm_w, *, batch, seq_len, tc, name):
    M, n_proj = proj.shape
    kd = wgk.shape[1]
    vd = norm_w.shape[0]
    H = GLA_HEADS
    dk, dv = kd // H, vd // H
    nt = seq_len // tc
    proj3 = proj.reshape(batch, seq_len, n_proj)
    y = pl.pallas_call(
        functools.partial(_gla_kernel, n_chunks=tc // CHUNK, q_scale=dk ** -0.5),
        grid=(H, nt),
        in_specs=[
            pl.BlockSpec((batch, tc, dk), lambda h, n: (0, n, h)),
            pl.BlockSpec((batch, tc, dk), lambda h, n: (0, n, kd // dk + h)),
            pl.BlockSpec((batch, tc, dv), lambda h, n: (0, n, 2 * kd // dv + h)),
            pl.BlockSpec((batch, tc, dv), lambda h, n: (0, n, (2 * kd + vd) // dv + h)),
            pl.BlockSpec((batch, tc, z.shape[1]), lambda h, n: (0, n, 0)),
            pl.BlockSpec((wgk.shape[0], dk), lambda h, n: (0, h)),
            pl.BlockSpec((1, dk), lambda h, n: (0, h)),
            pl.BlockSpec((1, dv), lambda h, n: (0, h)),
        ],
        out_specs=pl.BlockSpec((batch, tc, dv), lambda h, n: (0, n, h)),
        out_shape=jax.ShapeDtypeStruct((batch, seq_len, vd), BF16),
        scratch_shapes=[pltpu.VMEM((batch, dv, dk), F32)],
        compiler_params=_params(("arbitrary", "arbitrary")),
        name=name,
    )(proj3, proj3, proj3, proj3, z.reshape(batch, seq_len, z.shape[1]), wgk,
      bgk.reshape(1, kd), norm_w.reshape(1, vd))
    return y.reshape(M, vd)


def _rope_kernel(cos_ref, sin_ref):
    tt, half = cos_ref.shape
    pos = (pl.program_id(0) * tt + lax.broadcasted_iota(jnp.int32, (tt, half), 0)).astype(F32)
    jj = lax.broadcasted_iota(jnp.int32, (tt, half), 1).astype(F32)
    inv = jnp.exp(-(jj / half) * math.log(ROPE_BASE))
    ang = pos * inv
    cos_ref[...] = jnp.cos(ang)
    sin_ref[...] = jnp.sin(ang)


def _rope_tables(seq_len, half, *, tt):
    return pl.pallas_call(
        _rope_kernel,
        grid=(seq_len // tt,),
        out_specs=[pl.BlockSpec((tt, half), lambda i: (i, 0))] * 2,
        out_shape=[jax.ShapeDtypeStruct((seq_len, half), F32)] * 2,
        compiler_params=_params(("parallel",)),
        name="rope_tables",
    )()


def _ret_kernel(q_ref, k_ref, v_ref, g_ref, cos_ref, sin_ref, nw_ref, o_ref,
                st_ref, dmat_ref, xi_ref, zeta_ref, *, k_scale):
    tc, dk = q_ref.shape
    half = dk // 2
    head = pl.program_id(1).astype(F32)

    def log_gamma(shape):
        return jnp.log(1.0 - jnp.exp2(-5.0 - jnp.full(shape, head, F32)))

    @pl.when(pl.program_id(2) == 0)
    def _():
        st_ref[...] = jnp.zeros_like(st_ref)
        r, c, _ = _tile_masks(tc)
        shift = CHUNK.bit_length() - 1
        reads = lax.shift_right_logical(c, shift) <= lax.shift_right_logical(r, shift)
        dist = jnp.abs(r - c).astype(F32)
        dmat_ref[...] = jnp.where(reads, jnp.exp(log_gamma((tc, tc)) * dist), 0.0)
        pos = lax.broadcasted_iota(jnp.int32, (tc, dk), 0).astype(F32)
        lg = log_gamma((tc, dk))
        xi_ref[...] = jnp.exp(lg * (pos + 1.0))
        zeta_ref[...] = jnp.exp(lg * (tc - 1.0 - pos))

    decay = jnp.exp(log_gamma((1, dk)) * float(tc))

    def rope(x):
        x1, x2 = x[:, :half], x[:, half:]
        cos, sin = cos_ref[...], sin_ref[...]
        return jnp.concatenate([x1 * cos - x2 * sin, x2 * cos + x1 * sin], axis=-1)

    q = rope(q_ref[...].astype(F32))
    k = rope(k_ref[...].astype(F32)) * k_scale
    v = v_ref[...]
    scores = _dot_nt(q.astype(BF16), k.astype(BF16)) * dmat_ref[...]
    st = st_ref[...]
    o = _dot(scores.astype(BF16), v) + _dot_nt((q * xi_ref[...]).astype(BF16), st.astype(BF16))
    st_ref[...] = decay * st + _dot_tn(v, (k * zeta_ref[...]).astype(BF16))
    oc = o - jnp.mean(o, axis=-1, keepdims=True)
    o = oc * lax.rsqrt(jnp.mean(oc * oc, axis=-1, keepdims=True) + RMS_EPS) * nw_ref[...]
    g = g_ref[...].astype(F32)
    o_ref[...] = (o * (g * _sigmoid(g))).astype(o_ref.dtype)


def _ret_core(proj, cos, sin, norm_w, *, batch, seq_len, tc, name):
    M = proj.shape[0]
    vd = norm_w.shape[0]
    H = RET_HEADS
    dv = vd // H
    dk = cos.shape[1] * 2
    kd = H * dk
    nt = seq_len // tc
    row = lambda b, h, n: b * nt + n
    return pl.pallas_call(
        functools.partial(_ret_kernel, k_scale=dk ** -0.5),
        grid=(batch, H, nt),
        in_specs=[
            pl.BlockSpec((tc, dk), lambda b, h, n: (row(b, h, n), h)),
            pl.BlockSpec((tc, dk), lambda b, h, n: (row(b, h, n), H + h)),
            pl.BlockSpec((tc, dv), lambda b, h, n: (row(b, h, n), 2 * kd // dv + h)),
            pl.BlockSpec((tc, dv), lambda b, h, n: (row(b, h, n), (2 * kd + vd) // dv + h)),
            pl.BlockSpec((tc, dk // 2), lambda b, h, n: (n, 0)),
            pl.BlockSpec((tc, dk // 2), lambda b, h, n: (n, 0)),
            pl.BlockSpec((1, dv), lambda b, h, n: (0, h)),
        ],
        out_specs=pl.BlockSpec((tc, dv), lambda b, h, n: (row(b, h, n), h)),
        out_shape=jax.ShapeDtypeStruct((M, vd), BF16),
        scratch_shapes=[
            pltpu.VMEM((dv, dk), F32),
            pltpu.VMEM((tc, tc), F32),
            pltpu.VMEM((tc, dk), F32),
            pltpu.VMEM((tc, dk), F32),
        ],
        compiler_params=_params(("arbitrary", "arbitrary", "arbitrary")),
        name=name,
    )(proj, proj, proj, proj, cos, sin, norm_w.reshape(1, vd))


def _lru_kernel(xb_ref, yb_ref, cw_ref, cb_ref, wga_ref, bga_ref, wgx_ref, bgx_ref, lam_ref, o_ref,
                hcar_ref, ccar_ref, a_ref, u_ref, hs_ref, *, n_blocks):
    tt, width = xb_ref.shape
    kw = cw_ref.shape[0]

    @pl.when(pl.program_id(1) == 0)
    def _():
        hcar_ref[...] = jnp.zeros_like(hcar_ref)
        ccar_ref[...] = jnp.zeros_like(ccar_ref)

    bw = width // n_blocks
    for nb in range(n_blocks):
        cs = pl.ds(nb * bw, bw)
        xb = xb_ref[:, cs].astype(F32)
        ext = jnp.concatenate([ccar_ref[:, cs], xb], axis=0)
        ccar_ref[:, cs] = xb[tt - SUBLANES:, :]
        cw = cw_ref[:, cs]
        xc = cw[kw - 1:kw, :] * xb + cb_ref[:, cs]
        for d in range(1, kw):
            xc = xc + cw[kw - 1 - d:kw - d, :] * ext[SUBLANES - d:SUBLANES - d + tt, :]
        xcb = xc.astype(BF16)
        rg = _sigmoid(_dot(xcb, wga_ref[nb]) + bga_ref[:, cs])
        ig = _sigmoid(_dot(xcb, wgx_ref[nb]) + bgx_ref[:, cs])
        a = jnp.exp(rg * (-LRU_C * _softplus(-lam_ref[:, cs])))
        a_ref[:, cs] = a
        u_ref[:, cs] = xc * ig * jnp.sqrt(1.0 - a * a)

    def step(t, h):
        h = a_ref[pl.ds(t, 1), :] * h + u_ref[pl.ds(t, 1), :]
        hs_ref[pl.ds(t, 1), :] = h
        return h

    hcar_ref[0:1, :] = lax.fori_loop(0, tt, step, hcar_ref[0:1, :], unroll=8)

    for nb in range(n_blocks):
        cs = pl.ds(nb * bw, bw)
        o_ref[:, cs] = (hs_ref[:, cs] * _gelu_tanh(yb_ref[:, cs].astype(F32))).astype(o_ref.dtype)


def _lru_core(proj, conv_w, conv_b, w_ga, b_ga, w_gx, b_gx, lam, layer, *, batch, seq_len, tt, name):
    M = proj.shape[0]
    width = conv_w.shape[1]
    nt = seq_len // tt
    nb = w_ga.shape[1]
    vec = lambda: pl.BlockSpec((1, width), lambda b, n: (0, 0))
    wblk = lambda: pl.BlockSpec((None,) + w_ga.shape[1:], lambda b, n: (layer, 0, 0, 0))
    return pl.pallas_call(
        functools.partial(_lru_kernel, n_blocks=nb),
        grid=(batch, nt),
        in_specs=[
            pl.BlockSpec((tt, width), lambda b, n: (b * nt + n, 0)),
            pl.BlockSpec((tt, width), lambda b, n: (b * nt + n, 1)),
            pl.BlockSpec(conv_w.shape, lambda b, n: (0, 0)),
            vec(), wblk(), vec(), wblk(), vec(), vec(),
        ],
        out_specs=pl.BlockSpec((tt, width), lambda b, n: (b * nt + n, 0)),
        out_shape=jax.ShapeDtypeStruct((M, width), BF16),
        scratch_shapes=[
            pltpu.VMEM((SUBLANES, width), F32),
            pltpu.VMEM((SUBLANES, width), F32),
            pltpu.VMEM((tt, width), F32),
            pltpu.VMEM((tt, width), F32),
            pltpu.VMEM((tt, width), F32),
        ],
        compiler_params=_params(("arbitrary", "arbitrary")),
        name=name,
    )(proj, proj, conv_w, conv_b.reshape(1, width), w_ga, b_ga.reshape(1, width),
      w_gx, b_gx.reshape(1, width), lam.reshape(1, width))


def kernel(x, norm_mix_w, norm_ffn_w, norm_out_w, gla_w_in, gla_w_gk, gla_b_gk, gla_norm_w, gla_w_out, lru_w_in, lru_conv_w, lru_conv_b, lru_w_ga, lru_b_ga, lru_w_gx, lru_b_gx, lru_lambda, lru_w_out, ret_w_in, ret_norm_w, ret_w_out, ffn_w_up, ffn_conv_w, ffn_conv_b, ffn_w_down):
    B, S, D = x.shape
    M = B * S
    depth = norm_mix_w.shape[0]
    xf = x.reshape(M, D)

    gla_kd = gla_w_gk.shape[2]
    gla_vd = gla_norm_w.shape[1]
    gla_main = 2 * gla_kd + 2 * gla_vd
    rank = gla_w_gk.shape[1]
    ret_dk = ret_w_in.shape[2] // (6 * RET_HEADS)
    cos = sin = None
    tm_p = min(1024, M)
    tm_f = min(512, S)
    tc = min(256, S)
    tc_ret = min(512, S)

    gla_w_in_b, gla_w_out_b = gla_w_in.astype(BF16), gla_w_out.astype(BF16)
    lru_w_in_b, lru_w_out_b = lru_w_in.astype(BF16), lru_w_out.astype(BF16)
    lru_w_ga_b, lru_w_gx_b = lru_w_ga.astype(BF16), lru_w_gx.astype(BF16)
    ret_w_in_b, ret_w_out_b = ret_w_in.astype(BF16), ret_w_out.astype(BF16)
    ffn_w_up_b, ffn_w_down_b = ffn_w_up.astype(BF16), ffn_w_down.astype(BF16)

    for i in range(depth):
        kind, l = i % N_MIXERS, i // N_MIXERS
        if kind == 0:
            w_z = jnp.pad(gla_w_in[l][:, gla_main:].astype(BF16), ((0, 0), (0, LANES - rank)))
            w_gk = jnp.pad(gla_w_gk[l].astype(BF16), ((0, LANES - rank), (0, 0)))
            proj, z = _norm_proj(xf, norm_mix_w[i], gla_w_in_b, l, w_z, n_out=gla_main,
                                 tm=tm_p, tn=1024, name=f"gla_in_{i}")
            y = _gla_core(proj, z, w_gk, gla_b_gk[l], gla_norm_w[l],
                          batch=B, seq_len=S, tc=tc, name=f"gla_core_{i}")
            xf = _proj_res(y, gla_w_out_b, l, xf, tm=tm_p, tn=1024, name=f"gla_out_{i}")
        elif kind == 1:
            proj = _norm_proj(xf, norm_mix_w[i], lru_w_in_b, l, tm=tm_p, tn=1024, name=f"lru_in_{i}")
            y = _lru_core(proj, lru_conv_w[l], lru_conv_b[l], lru_w_ga_b, lru_b_ga[l],
                          lru_w_gx_b, lru_b_gx[l], lru_lambda[l], l,
                          batch=B, seq_len=S, tt=tc, name=f"lru_core_{i}")
            xf = _proj_res(y, lru_w_out_b, l, xf, tm=tm_p, tn=1024, name=f"lru_out_{i}")
        else:
            if cos is None:
                cos, sin = _rope_tables(S, ret_dk // 2, tt=tm_f)
            proj = _norm_proj(xf, norm_mix_w[i], ret_w_in_b, l, tm=tm_p, tn=1024, name=f"ret_in_{i}")
            y = _ret_core(proj, cos, sin, ret_norm_w[l],
                          batch=B, seq_len=S, tc=tc_ret, name=f"ret_core_{i}")
            xf = _proj_res(y, ret_w_out_b, l, xf, tm=tm_p, tn=512, name=f"ret_out_{i}")
        xf = _ffn(xf, norm_ffn_w[i], ffn_w_up_b, ffn_conv_w[i], ffn_conv_b[i], ffn_w_down_b, i,
                  norm_out_w if i == depth - 1 else None,
                  seq_len=S, tm=tm_f, tf=1024, sub=256, name=f"ffn_{i}")
    return xf.reshape(B, S, D)
```

```python
import functools
import math

import jax
import jax.numpy as jnp
from jax import lax
from jax.experimental import pallas as pl
from jax.experimental.pallas import tpu as pltpu

F32 = jnp.float32
BF16 = jnp.bfloat16

CHUNK = 64
RMS_EPS = 1e-6
N_MIXERS = 3
GLA_HEADS = 4
GLA_GATE_TAU = 16.0
LRU_BLOCK_W = 256
LRU_C = 8.0
RET_HEADS = 8
ROPE_BASE = 10000.0

LANES = 128
SUBLANES = 8
MXU_TILE = 256
ROW_BLK = 64
VMEM_LIMIT = 56 * 1024 * 1024


def _params(sem):
    return pltpu.CompilerParams(dimension_semantics=sem, vmem_limit_bytes=VMEM_LIMIT)


def _sigmoid(x):
    return 0.5 * jnp.tanh(0.5 * x) + 0.5


def _gelu_tanh(x):
    c = math.sqrt(2.0 / math.pi)
    return x * (0.5 * (1.0 + jnp.tanh(c * (x + 0.044715 * (x * x * x)))))


def _softplus(x):
    return jnp.maximum(x, 0.0) + jnp.log1p(jnp.exp(-jnp.abs(x)))


def _dot(a, b):
    return jnp.dot(a, b, preferred_element_type=F32)


def _dot_nt(a, b):
    return lax.dot_general(a, b, (((1,), (1,)), ((), ())), preferred_element_type=F32)


def _dot_tn(a, b):
    return lax.dot_general(a, b, (((0,), (0,)), ((), ())), preferred_element_type=F32)


def _norm_proj_kernel(x_ref, nw_ref, w_ref, *rest, has_extra):
    if has_extra:
        we_ref, o_ref, oe_ref, h_ref = rest
    else:
        o_ref, h_ref = rest

    @pl.when(pl.program_id(1) == 0)
    def _():
        x = x_ref[...]
        ms = jnp.mean(x * x, axis=-1, keepdims=True)
        h_ref[...] = (x * lax.rsqrt(ms + RMS_EPS) * nw_ref[...]).astype(BF16)
        if has_extra:
            oe_ref[...] = _dot(h_ref[...], we_ref[...]).astype(oe_ref.dtype)

    o_ref[...] = _dot(h_ref[...], w_ref[...]).astype(o_ref.dtype)


def _norm_proj(x, nw, w, layer, w_extra=None, *, n_out=None, tm, tn, name):
    M, D = x.shape
    N = w.shape[2] if n_out is None else n_out
    has_extra = w_extra is not None
    in_specs = [
        pl.BlockSpec((tm, D), lambda i, j: (i, 0)),
        pl.BlockSpec((1, D), lambda i, j: (0, 0)),
        pl.BlockSpec((None, D, tn), lambda i, j: (layer, 0, j)),
    ]
    args = [x, nw.reshape(1, D), w]
    out_shape = [jax.ShapeDtypeStruct((M, N), BF16)]
    out_specs = [pl.BlockSpec((tm, tn), lambda i, j: (i, j))]
    if has_extra:
        ne = w_extra.shape[1]
        in_specs.append(pl.BlockSpec((D, ne), lambda i, j: (0, 0)))
        args.append(w_extra)
        out_shape.append(jax.ShapeDtypeStruct((M, ne), BF16))
        out_specs.append(pl.BlockSpec((tm, ne), lambda i, j: (i, 0)))
    res = pl.pallas_call(
        functools.partial(_norm_proj_kernel, has_extra=has_extra),
        grid=(M // tm, N // tn),
        in_specs=in_specs,
        out_specs=out_specs,
        out_shape=out_shape,
        scratch_shapes=[pltpu.VMEM((tm, D), BF16)],
        compiler_params=_params(("parallel", "arbitrary")),
        name=name,
    )(*args)
    return res if has_extra else res[0]


def _proj_res_kernel(y_ref, w_ref, x_ref, o_ref):
    o_ref[...] = x_ref[...] + _dot(y_ref[...], w_ref[...])


def _proj_res(y, w, layer, x, *, tm, tn, name):
    M, K = y.shape
    N = w.shape[2]
    return pl.pallas_call(
        _proj_res_kernel,
        grid=(M // tm, N // tn),
        in_specs=[
            pl.BlockSpec((tm, K), lambda i, j: (i, 0)),
            pl.BlockSpec((None, K, tn), lambda i, j: (layer, 0, j)),
            pl.BlockSpec((tm, tn), lambda i, j: (i, j)),
        ],
        out_specs=pl.BlockSpec((tm, tn), lambda i, j: (i, j)),
        out_shape=jax.ShapeDtypeStruct((M, N), F32),
        compiler_params=_params(("parallel", "arbitrary")),
        name=name,
    )(y, w, x)


def _interleave(*lists):
    tagged = []
    for li, steps in enumerate(lists):
        tagged += [((k + 0.5) / len(steps), li, f) for k, f in enumerate(steps)]
    tagged.sort(key=lambda t: t[:2])
    return [f for _, _, f in tagged]


def _ffn_kernel(x_ref, nw_ref, wg_ref, wv_ref, cwg_ref, cwv_ref, cbg_ref, cbv_ref, wd_ref, *rest,
                tiles_per_seq, final_norm, sub):
    if final_norm:
        fnw_ref, o_ref, h_ref, carg_ref, carv_ref, ug_ref, uv_ref, act_ref = rest
    else:
        o_ref, h_ref, carg_ref, carv_ref, ug_ref, uv_ref, act_ref = rest
    i = pl.program_id(0)
    j = pl.program_id(1)
    nj = pl.num_programs(1)
    tm, dm = x_ref.shape
    tf = wd_ref.shape[0]
    nsub = tf // sub
    pad = SUBLANES

    @pl.when(j == 0)
    def _():
        x = x_ref[...]
        ms = jnp.mean(x * x, axis=-1, keepdims=True)
        h_ref[...] = (x * lax.rsqrt(ms + RMS_EPS) * nw_ref[...]).astype(BF16)
        o_ref[...] = x

    seq_start = (i % tiles_per_seq) == 0

    def up_steps(s):
        p = s % 2
        cs = pl.ds(s * sub, sub)
        acc = {}

        def kdot(w_ref, key, k):
            def f():
                ks = pl.ds(k * MXU_TILE, MXU_TILE)
                d = _dot(h_ref[:, ks], w_ref[ks, cs])
                acc[key] = d if k == 0 else acc[key] + d
            return f

        def park(key, u_ref, car_ref):
            def f():
                u = acc[key]
                u_ref[p, 0:pad, :] = jnp.where(seq_start, 0.0, car_ref[j, :, cs])
                car_ref[j, :, cs] = u[tm - pad:, :]
                u_ref[p, pad:pad + tm, :] = u
            return f

        steps = [kdot(wg_ref, "g", k) for k in range(dm // MXU_TILE)] + [park("g", ug_ref, carg_ref)]
        steps += [kdot(wv_ref, "v", k) for k in range(dm // MXU_TILE)] + [park("v", uv_ref, carv_ref)]
        return steps

    def gate_steps(s):
        p = s % 2
        cs = pl.ds(s * sub, sub)

        def conv(u_ref, cw_ref, cb_ref, r0):
            cw = cw_ref[:, cs]
            kw = cw.shape[0]
            y = cb_ref[:, cs] + cw[kw - 1:kw, :] * u_ref[p, pad + r0:pad + r0 + ROW_BLK, :]
            for d in range(1, kw):
                y = y + cw[kw - 1 - d:kw - d, :] * u_ref[p, pad + r0 - d:pad + r0 - d + ROW_BLK, :]
            return y

        def block(r0):
            def f():
                gate = conv(ug_ref, cwg_ref, cbg_ref, r0)
                val = conv(uv_ref, cwv_ref, cbv_ref, r0)
                act_ref[p, r0:r0 + ROW_BLK, :] = (_gelu_tanh(gate) * val).astype(BF16)
            return f

        return [block(r0) for r0 in range(0, tm, ROW_BLK)]

    def down_steps(s):
        p = s % 2
        cs = pl.ds(s * sub, sub)

        def ndot(n):
            def f():
                ns = pl.ds(n * MXU_TILE, MXU_TILE)
                o_ref[:, ns] += _dot(act_ref[p], wd_ref[cs, ns])
            return f

        return [ndot(n) for n in range(dm // MXU_TILE)]

    for f in up_steps(0):
        f()
    for s in range(nsub):
        slot = _interleave(gate_steps(s),
                           up_steps(s + 1) if s + 1 < nsub else [],
                           down_steps(s - 1) if s > 0 else [])
        for f in slot:
            f()
    for f in down_steps(nsub - 1):
        f()

    if final_norm:
        @pl.when(j == nj - 1)
        def _():
            y = o_ref[...]
            ms = jnp.mean(y * y, axis=-1, keepdims=True)
            o_ref[...] = y * lax.rsqrt(ms + RMS_EPS) * fnw_ref[...]


def _ffn(x, nw, w_up, conv_w, conv_b, w_down, layer, final_nw, *, seq_len, tm, tf, sub, name):
    M, D = x.shape
    F = w_down.shape[1]
    nj = F // tf
    final_norm = final_nw is not None
    in_specs = [
        pl.BlockSpec((tm, D), lambda i, j: (i, 0)),
        pl.BlockSpec((1, D), lambda i, j: (0, 0)),
        pl.BlockSpec((None, D, tf), lambda i, j: (layer, 0, j)),
        pl.BlockSpec((None, D, tf), lambda i, j: (layer, 0, nj + j)),
        pl.BlockSpec((conv_w.shape[0], tf), lambda i, j: (0, j)),
        pl.BlockSpec((conv_w.shape[0], tf), lambda i, j: (0, nj + j)),
        pl.BlockSpec((1, tf), lambda i, j: (0, j)),
        pl.BlockSpec((1, tf), lambda i, j: (0, nj + j)),
        pl.BlockSpec((None, tf, D), lambda i, j: (layer, j, 0)),
    ]
    cb = conv_b.reshape(1, 2 * F)
    args = [x, nw.reshape(1, D), w_up, w_up, conv_w, conv_w, cb, cb, w_down]
    if final_norm:
        in_specs.append(pl.BlockSpec((1, D), lambda i, j: (0, 0)))
        args.append(final_nw.reshape(1, D))
    return pl.pallas_call(
        functools.partial(_ffn_kernel, tiles_per_seq=seq_len // tm, final_norm=final_norm, sub=sub),
        grid=(M // tm, nj),
        in_specs=in_specs,
        out_specs=pl.BlockSpec((tm, D), lambda i, j: (i, 0)),
        out_shape=jax.ShapeDtypeStruct((M, D), F32),
        scratch_shapes=[
            pltpu.VMEM((tm, D), BF16),
            pltpu.VMEM((nj, SUBLANES, tf), F32),
            pltpu.VMEM((nj, SUBLANES, tf), F32),
            pltpu.VMEM((2, SUBLANES + tm, sub), F32),
            pltpu.VMEM((2, SUBLANES + tm, sub), F32),
            pltpu.VMEM((2, tm, sub), BF16),
        ],
        compiler_params=_params(("arbitrary", "arbitrary")),
        name=name,
    )(*args)


def _tile_masks(tc):
    r = lax.broadcasted_iota(jnp.int32, (tc, tc), 0)
    c = lax.broadcasted_iota(jnp.int32, (tc, tc), 1)
    shift = CHUNK.bit_length() - 1
    same = lax.shift_right_logical(r, shift) == lax.shift_right_logical(c, shift)
    return r, c, same


def _gla_kernel(q_ref, k_ref, v_ref, g_ref, z_ref, wgk_ref, bgk_ref, nw_ref, o_ref, st_ref, *,
                n_chunks, q_scale):
    @pl.when(pl.program_id(1) == 0)
    def _():
        st_ref[...] = jnp.zeros_like(st_ref)

    nb, tc, dk = q_ref.shape
    r, c, same = _tile_masks(tc)
    lower = same & (r >= c)
    upper = same & (r < c)
    tri = lower.astype(BF16)
    chunk_rows = [slice(ci * CHUNK, (ci + 1) * CHUNK) for ci in range(n_chunks)]

    def chain(bi):
        t = {}

        def gate():
            pre = _dot(z_ref[bi], wgk_ref[...]) + bgk_ref[...]
            log_a = -_softplus(-pre) / GLA_GATE_TAU
            t["hi"] = log_a.astype(BF16)
            t["lo"] = (log_a - t["hi"].astype(F32)).astype(BF16)

        def cumsum():
            b = _dot(tri, t["hi"]) + _dot(tri, t["lo"])
            t["b"] = b
            t["b_last"] = [b[(ci + 1) * CHUNK - 1:(ci + 1) * CHUNK, :] for ci in range(n_chunks)]

        def decays():
            b = t["b"]
            b_last_rows = jnp.concatenate(
                [jnp.broadcast_to(bl, (CHUNK, dk)) for bl in t["b_last"]], axis=0)
            eb = jnp.exp(b)
            enb = jnp.exp(-b)
            q = q_ref[bi].astype(F32) * q_scale
            k = k_ref[bi].astype(F32)
            t["q_dec"] = (q * eb).astype(BF16)
            t["k_grow"] = (k * enb).astype(BF16)
            t["q_grow"] = (q * enb).astype(BF16)
            t["k_dec"] = (k * eb).astype(BF16)
            t["k_in"] = (k * jnp.exp(b_last_rows - b)).astype(BF16)

        def intra():
            scores = jnp.where(lower, _dot_nt(t["q_dec"], t["k_grow"]),
                               jnp.where(upper, _dot_nt(t["q_grow"], t["k_dec"]), 0.0))
            t["o_intra"] = _dot(scores.astype(BF16), v_ref[bi])

        def state():
            v = v_ref[bi]
            incs = [_dot_tn(v[rows, :], t["k_in"][rows, :]) for rows in chunk_rows]
            states = [st_ref[bi]]
            for ci in range(n_chunks):
                states.append(jnp.exp(t["b_last"][ci]) * states[ci] + incs[ci])
            st_ref[bi] = states[n_chunks]
            t["states"] = states

        def readout():
            o = jnp.concatenate(
                [t["o_intra"][rows, :] + _dot_nt(t["q_dec"][rows, :], t["states"][ci].astype(BF16))
                 for ci, rows in enumerate(chunk_rows)], axis=0)
            o = o * lax.rsqrt(jnp.mean(o * o, axis=-1, keepdims=True) + RMS_EPS) * nw_ref[...]
            g = g_ref[bi].astype(F32)
            o_ref[bi] = (o * (g * _sigmoid(g))).astype(o_ref.dtype)

        return [gate, cumsum, decays, intra, state, readout]

    for f in _interleave(*[chain(bi) for bi in range(nb)]):
        f()


def _gla_core(proj, z, wgk, bgk, norm_w, *, batch, seq_len, tc, name):
    M, n_proj = proj.shape
    kd = wgk.shape[1]
    vd = norm_w.shape[0]
    H = GLA_HEADS
    dk, dv = kd // H, vd // H
    nt = seq_len // tc
    proj3 = proj.reshape(batch, seq_len, n_proj)
    y = pl.pallas_call(
        functools.partial(_gla_kernel, n_chunks=tc // CHUNK, q_scale=dk ** -0.5),
        grid=(H, nt),
        in_specs=[
            pl.BlockSpec((batch, tc, dk), lambda h, n: (0, n, h)),
            pl.BlockSpec((batch, tc, dk), lambda h, n: (0, n, kd // dk + h)),
            pl.BlockSpec((batch, tc, dv), lambda h, n: (0, n, 2 * kd // dv + h)),
            pl.BlockSpec((batch, tc, dv), lambda h, n: (0, n, (2 * kd + vd) // dv + h)),
            pl.BlockSpec((batch, tc, z.shape[1]), lambda h, n: (0, n, 0)),
            pl.BlockSpec((wgk.shape[0], dk), lambda h, n: (0, h)),
            pl.BlockSpec((1, dk), lambda h, n: (0, h)),
            pl.BlockSpec((1, dv), lambda h, n: (0, h)),
        ],
        out_specs=pl.BlockSpec((batch, tc, dv), lambda h, n: (0, n, h)),
        out_shape=jax.ShapeDtypeStruct((batch, seq_len, vd), BF16),
        scratch_shapes=[pltpu.VMEM((batch, dv, dk), F32)],
        compiler_params=_params(("arbitrary", "arbitrary")),
        name=name,
    )(proj3, proj3, proj3, proj3, z.reshape(batch, seq_len, z.shape[1]), wgk,
      bgk.reshape(1, kd), norm_w.reshape(1, vd))
    return y.reshape(M, vd)


def _gla_layer_kernel(x_ref, nw_ref, w_ref, wz_ref, wgk_ref, bgk_ref, gnw_ref, o_ref,
                      h_ref, st_ref, z_ref, q_ref, k_ref, v_ref, g_ref, *, heads, q_scale):
    nb, tc, dm = x_ref.shape
    kd = wgk_ref.shape[1]
    vd = o_ref.shape[2]
    dk, dv = kd // heads, vd // heads
    n_chunks = tc // CHUNK

    @pl.when(pl.program_id(0) == 0)
    def _():
        st_ref[...] = jnp.zeros_like(st_ref)

    for bi in range(nb):
        x = x_ref[bi]
        ms = jnp.mean(x * x, axis=-1, keepdims=True)
        h_ref[bi * tc:(bi + 1) * tc, :] = (x * lax.rsqrt(ms + RMS_EPS) * nw_ref[...]).astype(BF16)
    z_ref[...] = _dot(h_ref[...], wz_ref[...]).astype(BF16)

    r, c, same = _tile_masks(tc)
    lower = same & (r >= c)
    upper = same & (r < c)
    tri = lower.astype(BF16)
    chunk_rows = [slice(ci * CHUNK, (ci + 1) * CHUNK) for ci in range(n_chunks)]

    def proj_steps(hd):
        p = hd % 2
        steps = []
        for dst, col0, width in ((q_ref, hd * dk, dk), (k_ref, kd + hd * dk, dk),
                                 (v_ref, 2 * kd + hd * dv, dv), (g_ref, 2 * kd + vd + hd * dv, dv)):
            for c0 in range(0, width, MXU_TILE):
                acc = {}

                def kdot(kb, col=col0 + c0, acc=acc):
                    def f():
                        ks = pl.ds(kb * MXU_TILE, MXU_TILE)
                        d = _dot(h_ref[:, ks], w_ref[ks, pl.ds(col, MXU_TILE)])
                        acc["v"] = d if kb == 0 else acc["v"] + d
                    return f

                def park(dst=dst, c0=c0, acc=acc):
                    def f():
                        dst[p, :, c0:c0 + MXU_TILE] = acc["v"].astype(dst.dtype)
                    return f

                steps += [kdot(kb) for kb in range(dm // MXU_TILE)] + [park()]
        return steps

    def chain(hd, bi):
        p = hd % 2
        rows_b = slice(bi * tc, (bi + 1) * tc)
        kcols = slice(hd * dk, (hd + 1) * dk)
        vcols = slice(hd * dv, (hd + 1) * dv)
        t = {}

        def gate():
            pre = _dot(z_ref[rows_b, :], wgk_ref[:, kcols]) + bgk_ref[:, kcols]
            log_a = -_softplus(-pre) / GLA_GATE_TAU
            t["hi"] = log_a.astype(BF16)
            t["lo"] = (log_a - t["hi"].astype(F32)).astype(BF16)

        def cumsum():
            b = _dot(tri, t["hi"]) + _dot(tri, t["lo"])
            t["b"] = b
            t["b_last"] = [b[(ci + 1) * CHUNK - 1:(ci + 1) * CHUNK, :] for ci in range(n_chunks)]

        def decays():
            b = t["b"]
            b_last_rows = jnp.concatenate(
                [jnp.broadcast_to(bl, (CHUNK, dk)) for bl in t["b_last"]], axis=0)
            eb = jnp.exp(b)
            enb = jnp.exp(-b)
            q = q_ref[p, rows_b, :] * q_scale
            k = k_ref[p, rows_b, :]
            t["q_dec"] = (q * eb).astype(BF16)
            t["k_grow"] = (k * enb).astype(BF16)
            t["q_grow"] = (q * enb).astype(BF16)
            t["k_dec"] = (k * eb).astype(BF16)
            t["k_in"] = (k * jnp.exp(b_last_rows - b)).astype(BF16)

        def intra():
            scores = jnp.where(lower, _dot_nt(t["q_dec"], t["k_grow"]),
                               jnp.where(upper, _dot_nt(t["q_grow"], t["k_dec"]), 0.0))
            t["o_intra"] = _dot(scores.astype(BF16), v_ref[p, rows_b, :])

        def state():
            v = v_ref[p, rows_b, :]
            incs = [_dot_tn(v[rows, :], t["k_in"][rows, :]) for rows in chunk_rows]
            states = [st_ref[bi, hd]]
            for ci in range(n_chunks):
                states.append(jnp.exp(t["b_last"][ci]) * states[ci] + incs[ci])
            st_ref[bi, hd] = states[n_chunks]
            t["states"] = states

        def readout():
            o = jnp.concatenate(
                [t["o_intra"][rows, :] + _dot_nt(t["q_dec"][rows, :], t["states"][ci].astype(BF16))
                 for ci, rows in enumerate(chunk_rows)], axis=0)
            o = o * lax.rsqrt(jnp.mean(o * o, axis=-1, keepdims=True) + RMS_EPS) * gnw_ref[:, vcols]
            g = g_ref[p, rows_b, :]
            o_ref[bi, :, vcols] = (o * (g * _sigmoid(g))).astype(o_ref.dtype)

        return [gate, cumsum, decays, intra, state, readout]

    for f in proj_steps(0):
        f()
    for hd in range(heads):
        slot = [chain(hd, bi) for bi in range(nb)]
        if hd + 1 < heads:
            slot.append(proj_steps(hd + 1))
        for f in _interleave(*slot):
            f()


def _gla_layer(x3, nw, w_in, layer, w_z, wgk, bgk, gnorm_w, *, tc, name):
    B, S, D = x3.shape
    kd = wgk.shape[1]
    vd = gnorm_w.shape[0]
    H = GLA_HEADS
    dk, dv = kd // H, vd // H
    rows = B * tc
    const = lambda n: (0, 0)
    return pl.pallas_call(
        functools.partial(_gla_layer_kernel, heads=H, q_scale=dk ** -0.5),
        grid=(S // tc,),
        in_specs=[
            pl.BlockSpec((B, tc, D), lambda n: (0, n, 0)),
            pl.BlockSpec((1, D), const),
            pl.BlockSpec((None, D, 2 * kd + 2 * vd), lambda n: (layer, 0, 0),
                         pipeline_mode=pl.Buffered(1)),
            pl.BlockSpec(w_z.shape, const),
            pl.BlockSpec(wgk.shape, const),
            pl.BlockSpec((1, kd), const),
            pl.BlockSpec((1, vd), const),
        ],
        out_specs=pl.BlockSpec((B, tc, vd), lambda n: (0, n, 0)),
        out_shape=jax.ShapeDtypeStruct((B, S, vd), BF16),
        scratch_shapes=[
            pltpu.VMEM((rows, D), BF16),
            pltpu.VMEM((B, H, dv, dk), F32),
            pltpu.VMEM((rows, w_z.shape[1]), BF16),
            pltpu.VMEM((2, rows, dk), F32),
            pltpu.VMEM((2, rows, dk), F32),
            pltpu.VMEM((2, rows, dv), BF16),
            pltpu.VMEM((2, rows, dv), F32),
        ],
        compiler_params=_params(("arbitrary",)),
        name=name,
    )(x3, nw.reshape(1, D), w_in, w_z, wgk, bgk.reshape(1, kd), gnorm_w.reshape(1, vd))


def _rope_kernel(cos_ref, sin_ref):
    tt, half = cos_ref.shape
    pos = (pl.program_id(0) * tt + lax.broadcasted_iota(jnp.int32, (tt, half), 0)).astype(F32)
    jj = lax.broadcasted_iota(jnp.int32, (tt, half), 1).astype(F32)
    inv = jnp.exp(-(jj / half) * math.log(ROPE_BASE))
    ang = pos * inv
    cos_ref[...] = jnp.cos(ang)
    sin_ref[...] = jnp.sin(ang)


def _rope_tables(seq_len, half, *, tt):
    return pl.pallas_call(
        _rope_kernel,
        grid=(seq_len // tt,),
        out_specs=[pl.BlockSpec((tt, half), lambda i: (i, 0))] * 2,
        out_shape=[jax.ShapeDtypeStruct((seq_len, half), F32)] * 2,
        compiler_params=_params(("parallel",)),
        name="rope_tables",
    )()


def _ret_kernel(q_ref, k_ref, v_ref, g_ref, cos_ref, sin_ref, nw_ref, o_ref,
                st_ref, dmat_ref, xi_ref, zeta_ref, *, k_scale):
    tc, dk = q_ref.shape
    half = dk // 2
    head = pl.program_id(1).astype(F32)

    def log_gamma(shape):
        return jnp.log(1.0 - jnp.exp2(-5.0 - jnp.full(shape, head, F32)))

    @pl.when(pl.program_id(2) == 0)
    def _():
        st_ref[...] = jnp.zeros_like(st_ref)
        r, c, _ = _tile_masks(tc)
        shift = CHUNK.bit_length() - 1
        reads = lax.shift_right_logical(c, shift) <= lax.shift_right_logical(r, shift)
        dist = jnp.abs(r - c).astype(F32)
        dmat_ref[...] = jnp.where(reads, jnp.exp(log_gamma((tc, tc)) * dist), 0.0)
        pos = lax.broadcasted_iota(jnp.int32, (tc, dk), 0).astype(F32)
        lg = log_gamma((tc, dk))
        xi_ref[...] = jnp.exp(lg * (pos + 1.0))
        zeta_ref[...] = jnp.exp(lg * (tc - 1.0 - pos))

    decay = jnp.exp(log_gamma((1, dk)) * float(tc))

    def rope(x):
        x1, x2 = x[:, :half], x[:, half:]
        cos, sin = cos_ref[...], sin_ref[...]
        return jnp.concatenate([x1 * cos - x2 * sin, x2 * cos + x1 * sin], axis=-1)

    q = rope(q_ref[...].astype(F32))
    k = rope(k_ref[...].astype(F32)) * k_scale
    v = v_ref[...]
    scores = _dot_nt(q.astype(BF16), k.astype(BF16)) * dmat_ref[...]
    st = st_ref[...]
    o = _dot(scores.astype(BF16), v) + _dot_nt((q * xi_ref[...]).astype(BF16), st.astype(BF16))
    st_ref[...] = decay * st + _dot_tn(v, (k * zeta_ref[...]).astype(BF16))
    oc = o - jnp.mean(o, axis=-1, keepdims=True)
    o = oc * lax.rsqrt(jnp.mean(oc * oc, axis=-1, keepdims=True) + RMS_EPS) * nw_ref[...]
    g = g_ref[...].astype(F32)
    o_ref[...] = (o * (g * _sigmoid(g))).astype(o_ref.dtype)


def _ret_core(proj, cos, sin, norm_w, *, batch, seq_len, tc, name):
    M = proj.shape[0]
    vd = norm_w.shape[0]
    H = RET_HEADS
    dv = vd // H
    dk = cos.shape[1] * 2
    kd = H * dk
    nt = seq_len // tc
    row = lambda b, h, n: b * nt + n
    return pl.pallas_call(
        functools.partial(_ret_kernel, k_scale=dk ** -0.5),
        grid=(batch, H, nt),
        in_specs=[
            pl.BlockSpec((tc, dk), lambda b, h, n: (row(b, h, n), h)),
            pl.BlockSpec((tc, dk), lambda b, h, n: (row(b, h, n), H + h)),
            pl.BlockSpec((tc, dv), lambda b, h, n: (row(b, h, n), 2 * kd // dv + h)),
            pl.BlockSpec((tc, dv), lambda b, h, n: (row(b, h, n), (2 * kd + vd) // dv + h)),
            pl.BlockSpec((tc, dk // 2), lambda b, h, n: (n, 0)),
            pl.BlockSpec((tc, dk // 2), lambda b, h, n: (n, 0)),
            pl.BlockSpec((1, dv), lambda b, h, n: (0, h)),
        ],
        out_specs=pl.BlockSpec((tc, dv), lambda b, h, n: (row(b, h, n), h)),
        out_shape=jax.ShapeDtypeStruct((M, vd), BF16),
        scratch_shapes=[
            pltpu.VMEM((dv, dk), F32),
            pltpu.VMEM((tc, tc), F32),
            pltpu.VMEM((tc, dk), F32),
            pltpu.VMEM((tc, dk), F32),
        ],
        compiler_params=_params(("arbitrary", "arbitrary", "arbitrary")),
        name=name,
    )(proj, proj, proj, proj, cos, sin, norm_w.reshape(1, vd))


def _lru_kernel(xb_ref, yb_ref, cw_ref, cb_ref, wga_ref, bga_ref, wgx_ref, bgx_ref, lam_ref, o_ref,
                hcar_ref, ccar_ref, a_ref, u_ref, hs_ref, *, n_blocks):
    tt, width = xb_ref.shape
    kw = cw_ref.shape[0]

    @pl.when(pl.program_id(1) == 0)
    def _():
        hcar_ref[...] = jnp.zeros_like(hcar_ref)
        ccar_ref[...] = jnp.zeros_like(ccar_ref)

    bw = width // n_blocks
    for nb in range(n_blocks):
        cs = pl.ds(nb * bw, bw)
        xb = xb_ref[:, cs].astype(F32)
        ext = jnp.concatenate([ccar_ref[:, cs], xb], axis=0)
        ccar_ref[:, cs] = xb[tt - SUBLANES:, :]
        cw = cw_ref[:, cs]
        xc = cw[kw - 1:kw, :] * xb + cb_ref[:, cs]
        for d in range(1, kw):
            xc = xc + cw[kw - 1 - d:kw - d, :] * ext[SUBLANES - d:SUBLANES - d + tt, :]
        xcb = xc.astype(BF16)
        rg = _sigmoid(_dot(xcb, wga_ref[nb]) + bga_ref[:, cs])
        ig = _sigmoid(_dot(xcb, wgx_ref[nb]) + bgx_ref[:, cs])
        a = jnp.exp(rg * (-LRU_C * _softplus(-lam_ref[:, cs])))
        a_ref[:, cs] = a
        u_ref[:, cs] = xc * ig * jnp.sqrt(1.0 - a * a)

    def step(t, h):
        h = a_ref[pl.ds(t, 1), :] * h + u_ref[pl.ds(t, 1), :]
        hs_ref[pl.ds(t, 1), :] = h
        return h

    hcar_ref[0:1, :] = lax.fori_loop(0, tt, step, hcar_ref[0:1, :], unroll=8)

    for nb in range(n_blocks):
        cs = pl.ds(nb * bw, bw)
        o_ref[:, cs] = (hs_ref[:, cs] * _gelu_tanh(yb_ref[:, cs].astype(F32))).astype(o_ref.dtype)


def _lru_core(proj, conv_w, conv_b, w_ga, b_ga, w_gx, b_gx, lam, layer, *, batch, seq_len, tt, name):
    M = proj.shape[0]
    width = conv_w.shape[1]
    nt = seq_len // tt
    nb = w_ga.shape[1]
    vec = lambda: pl.BlockSpec((1, width), lambda b, n: (0, 0))
    wblk = lambda: pl.BlockSpec((None,) + w_ga.shape[1:], lambda b, n: (layer, 0, 0, 0))
    return pl.pallas_call(
        functools.partial(_lru_kernel, n_blocks=nb),
        grid=(batch, nt),
        in_specs=[
            pl.BlockSpec((tt, width), lambda b, n: (b * nt + n, 0)),
            pl.BlockSpec((tt, width), lambda b, n: (b * nt + n, 1)),
            pl.BlockSpec(conv_w.shape, lambda b, n: (0, 0)),
            vec(), wblk(), vec(), wblk(), vec(), vec(),
        ],
        out_specs=pl.BlockSpec((tt, width), lambda b, n: (b * nt + n, 0)),
        out_shape=jax.ShapeDtypeStruct((M, width), BF16),
        scratch_shapes=[
            pltpu.VMEM((SUBLANES, width), F32),
            pltpu.VMEM((SUBLANES, width), F32),
            pltpu.VMEM((tt, width), F32),
            pltpu.VMEM((tt, width), F32),
            pltpu.VMEM((tt, width), F32),
        ],
        compiler_params=_params(("arbitrary", "arbitrary")),
        name=name,
    )(proj, proj, conv_w, conv_b.reshape(1, width), w_ga, b_ga.reshape(1, width),
      w_gx, b_gx.reshape(1, width), lam.reshape(1, width))


def _lru_layer_kernel(x_ref, nw_ref, w_ref, cw_ref, cb_ref, wga_ref, bga_ref, wgx_ref, bgx_ref,
                      lam_ref, o_ref, h_ref, hcar_ref, ccar_ref, xb_ref, yb_ref, a_ref, u_ref, hs_ref,
                      gy_ref, *, n_blocks):
    tt, dm = x_ref.shape
    width = o_ref.shape[1]
    kw = cw_ref.shape[0]
    bw = width // n_blocks

    @pl.when(pl.program_id(1) == 0)
    def _():
        hcar_ref[...] = jnp.zeros_like(hcar_ref)
        ccar_ref[...] = jnp.zeros_like(ccar_ref)

    x = x_ref[...]
    ms = jnp.mean(x * x, axis=-1, keepdims=True)
    h_ref[...] = (x * lax.rsqrt(ms + RMS_EPS) * nw_ref[...]).astype(BF16)

    def proj_steps(nb):
        p = nb % 2
        steps = []
        for dst, col in ((xb_ref, nb * bw), (yb_ref, width + nb * bw)):
            acc = {}

            def kdot(kb, col=col, acc=acc):
                def f():
                    ks = pl.ds(kb * MXU_TILE, MXU_TILE)
                    d = _dot(h_ref[:, ks], w_ref[ks, pl.ds(col, bw)])
                    acc["v"] = d if kb == 0 else acc["v"] + d
                return f

            def park(dst=dst, acc=acc):
                def f():
                    dst[p] = acc["v"]
                return f

            steps += [kdot(kb) for kb in range(dm // MXU_TILE)] + [park()]
        return steps

    def gate_steps(nb):
        p = nb % 2
        cs = pl.ds(nb * bw, bw)
        t = {}

        def conv():
            xb = xb_ref[p]
            ext = jnp.concatenate([ccar_ref[:, cs], xb], axis=0)
            ccar_ref[:, cs] = xb[tt - SUBLANES:, :]
            cw = cw_ref[:, cs]
            xc = cw[kw - 1:kw, :] * xb + cb_ref[:, cs]
            for d in range(1, kw):
                xc = xc + cw[kw - 1 - d:kw - d, :] * ext[SUBLANES - d:SUBLANES - d + tt, :]
            t["xc"] = xc

        def gates():
            xc = t["xc"]
            xcb = xc.astype(BF16)
            rg = _sigmoid(_dot(xcb, wga_ref[nb]) + bga_ref[:, cs])
            ig = _sigmoid(_dot(xcb, wgx_ref[nb]) + bgx_ref[:, cs])
            a = jnp.exp(rg * (-LRU_C * _softplus(-lam_ref[:, cs])))
            a_ref[:, cs] = a
            y = 1.0 - a * a
            u_ref[:, cs] = xc * ig * jnp.where(y > 0.0, y * lax.rsqrt(y), 0.0)

        def out_gate():
            gy_ref[:, cs] = _gelu_tanh(yb_ref[p])

        return [conv, gates, out_gate]

    for f in proj_steps(0):
        f()
    for nb in range(n_blocks):
        slot = [gate_steps(nb)] + ([proj_steps(nb + 1)] if nb + 1 < n_blocks else [])
        for f in _interleave(*slot):
            f()

    def step(t, h):
        h = a_ref[pl.ds(t, 1), :] * h + u_ref[pl.ds(t, 1), :]
        hs_ref[pl.ds(t, 1), :] = h
        return h

    hcar_ref[0:1, :] = lax.fori_loop(0, tt, step, hcar_ref[0:1, :], unroll=8)

    for nb in range(n_blocks):
        cs = pl.ds(nb * bw, bw)
        o_ref[:, cs] = (hs_ref[:, cs] * gy_ref[:, cs]).astype(o_ref.dtype)


def _lru_layer(x, nw, w_in, conv_w, conv_b, w_ga, b_ga, w_gx, b_gx, lam, layer, *, batch, seq_len, tt,
               name):
    M, D = x.shape
    width = conv_w.shape[1]
    nt = seq_len // tt
    nb = w_ga.shape[1]
    bw = width // nb
    vec = lambda: pl.BlockSpec((1, width), lambda b, n: (0, 0))
    wblk = lambda: pl.BlockSpec((None,) + w_ga.shape[1:], lambda b, n: (layer, 0, 0, 0))
    return pl.pallas_call(
        functools.partial(_lru_layer_kernel, n_blocks=nb),
        grid=(batch, nt),
        in_specs=[
            pl.BlockSpec((tt, D), lambda b, n: (b * nt + n, 0)),
            pl.BlockSpec((1, D), lambda b, n: (0, 0)),
            pl.BlockSpec((None, D, 2 * width), lambda b, n: (layer, 0, 0),
                         pipeline_mode=pl.Buffered(1)),
            pl.BlockSpec(conv_w.shape, lambda b, n: (0, 0)),
            vec(), wblk(), vec(), wblk(), vec(), vec(),
        ],
        out_specs=pl.BlockSpec((tt, width), lambda b, n: (b * nt + n, 0)),
        out_shape=jax.ShapeDtypeStruct((M, width), BF16),
        scratch_shapes=[
            pltpu.VMEM((tt, D), BF16),
            pltpu.VMEM((SUBLANES, width), F32),
            pltpu.VMEM((SUBLANES, width), F32),
            pltpu.VMEM((2, tt, bw), F32),
            pltpu.VMEM((2, tt, bw), F32),
            pltpu.VMEM((tt, width), F32),
            pltpu.VMEM((tt, width), F32),
            pltpu.VMEM((tt, width), F32),
            pltpu.VMEM((tt, width), F32),
        ],
        compiler_params=_params(("arbitrary", "arbitrary")),
        name=name,
    )(x, nw.reshape(1, D), w_in, conv_w, conv_b.reshape(1, width), w_ga, b_ga.reshape(1, width),
      w_gx, b_gx.reshape(1, width), lam.reshape(1, width))


def kernel(x, norm_mix_w, norm_ffn_w, norm_out_w, gla_w_in, gla_w_gk, gla_b_gk, gla_norm_w, gla_w_out, lru_w_in, lru_conv_w, lru_conv_b, lru_w_ga, lru_b_ga, lru_w_gx, lru_b_gx, lru_lambda, lru_w_out, ret_w_in, ret_norm_w, ret_w_out, ffn_w_up, ffn_conv_w, ffn_conv_b, ffn_w_down):
    B, S, D = x.shape
    M = B * S
    depth = norm_mix_w.shape[0]
    xf = x.reshape(M, D)

    gla_kd = gla_w_gk.shape[2]
    gla_vd = gla_norm_w.shape[1]
    gla_main = 2 * gla_kd + 2 * gla_vd
    rank = gla_w_gk.shape[1]
    ret_dk = ret_w_in.shape[2] // (6 * RET_HEADS)
    cos = sin = None
    tm_p = min(1024, M)
    tm_f = min(512, S)
    tc = min(256, S)
    tt_lru = min(512, S)
    tc_ret = min(512, S)

    gla_w_in_b, gla_w_out_b = gla_w_in[:, :, :gla_main].astype(BF16), gla_w_out.astype(BF16)
    lru_w_in_b, lru_w_out_b = lru_w_in.astype(BF16), lru_w_out.astype(BF16)
    lru_w_ga_b, lru_w_gx_b = lru_w_ga.astype(BF16), lru_w_gx.astype(BF16)
    ret_w_in_b, ret_w_out_b = ret_w_in.astype(BF16), ret_w_out.astype(BF16)
    ffn_w_up_b, ffn_w_down_b = ffn_w_up.astype(BF16), ffn_w_down.astype(BF16)

    for i in range(depth):
        kind, l = i % N_MIXERS, i // N_MIXERS
        if kind == 0:
            w_z = jnp.pad(gla_w_in[l][:, gla_main:].astype(BF16), ((0, 0), (0, LANES - rank)))
            w_gk = jnp.pad(gla_w_gk[l].astype(BF16), ((0, LANES - rank), (0, 0)))
            y = _gla_layer(xf.reshape(B, S, D), norm_mix_w[i], gla_w_in_b, l, w_z, w_gk, gla_b_gk[l],
                           gla_norm_w[l], tc=tc, name=f"gla_mix_{i}")
            xf = _proj_res(y.reshape(M, gla_vd), gla_w_out_b, l, xf, tm=tm_p, tn=1024,
                           name=f"gla_out_{i}")
        elif kind == 1:
            y = _lru_layer(xf, norm_mix_w[i], lru_w_in_b, lru_conv_w[l], lru_conv_b[l], lru_w_ga_b,
                           lru_b_ga[l], lru_w_gx_b, lru_b_gx[l], lru_lambda[l], l,
                           batch=B, seq_len=S, tt=tt_lru, name=f"lru_mix_{i}")
            xf = _proj_res(y, lru_w_out_b, l, xf, tm=tm_p, tn=1024, name=f"lru_out_{i}")
        else:
            if cos is None:
                cos, sin = _rope_tables(S, ret_dk // 2, tt=tm_f)
            proj = _norm_proj(xf, norm_mix_w[i], ret_w_in_b, l, tm=tm_p, tn=1024, name=f"ret_in_{i}")
            y = _ret_core(proj, cos, sin, ret_norm_w[l],
                          batch=B, seq_len=S, tc=tc_ret, name=f"ret_core_{i}")
            xf = _proj_res(y, ret_w_out_b, l, xf, tm=tm_p, tn=512, name=f"ret_out_{i}")
        xf = _ffn(xf, norm_ffn_w[i], ffn_w_up_b, ffn_conv_w[i], ffn_conv_b[i], ffn_w_down_b, i,
                  norm_out_w if i == depth - 1 else None,
                  seq_len=S, tm=tm_f, tf=1024, sub=256, name=f"ffn_{i}")
    return xf.reshape(B, S, D)
```

```python
import functools
import math

import jax
import jax.numpy as jnp
from jax import lax
from jax.experimental import pallas as pl
from jax.experimental.pallas import tpu as pltpu

F32 = jnp.float32
BF16 = jnp.bfloat16

CHUNK = 64
RMS_EPS = 1e-6
N_MIXERS = 3
GLA_HEADS = 4
GLA_GATE_TAU = 16.0
LRU_BLOCK_W = 256
LRU_C = 8.0
RET_HEADS = 8
ROPE_BASE = 10000.0

LANES = 128
SUBLANES = 8
MXU_TILE = 256
ROW_BLK = 64
VMEM_LIMIT = 56 * 1024 * 1024


def _params(sem):
    return pltpu.CompilerParams(dimension_semantics=sem, vmem_limit_bytes=VMEM_LIMIT)


def _sigmoid(x):
    return 0.5 * jnp.tanh(0.5 * x) + 0.5


def _gelu_tanh(x):
    c = math.sqrt(2.0 / math.pi)
    return x * (0.5 * (1.0 + jnp.tanh(c * (x + 0.044715 * (x * x * x)))))


def _softplus(x):
    return jnp.maximum(x, 0.0) + jnp.log1p(jnp.exp(-jnp.abs(x)))


def _dot(a, b):
    return jnp.dot(a, b, preferred_element_type=F32)


def _dot_nt(a, b):
    return lax.dot_general(a, b, (((1,), (1,)), ((), ())), preferred_element_type=F32)


def _dot_tn(a, b):
    return lax.dot_general(a, b, (((0,), (0,)), ((), ())), preferred_element_type=F32)


def _norm_proj_kernel(x_ref, nw_ref, w_ref, *rest, has_extra):
    if has_extra:
        we_ref, o_ref, oe_ref, h_ref = rest
    else:
        o_ref, h_ref = rest

    @pl.when(pl.program_id(1) == 0)
    def _():
        x = x_ref[...]
        ms = jnp.mean(x * x, axis=-1, keepdims=True)
        h_ref[...] = (x * lax.rsqrt(ms + RMS_EPS) * nw_ref[...]).astype(BF16)
        if has_extra:
            oe_ref[...] = _dot(h_ref[...], we_ref[...]).astype(oe_ref.dtype)

    o_ref[...] = _dot(h_ref[...], w_ref[...]).astype(o_ref.dtype)


def _norm_proj(x, nw, w, layer, w_extra=None, *, n_out=None, tm, tn, name):
    M, D = x.shape
    N = w.shape[2] if n_out is None else n_out
    has_extra = w_extra is not None
    in_specs = [
        pl.BlockSpec((tm, D), lambda i, j: (i, 0)),
        pl.BlockSpec((1, D), lambda i, j: (0, 0)),
        pl.BlockSpec((None, D, tn), lambda i, j: (layer, 0, j)),
    ]
    args = [x, nw.reshape(1, D), w]
    out_shape = [jax.ShapeDtypeStruct((M, N), BF16)]
    out_specs = [pl.BlockSpec((tm, tn), lambda i, j: (i, j))]
    if has_extra:
        ne = w_extra.shape[1]
        in_specs.append(pl.BlockSpec((D, ne), lambda i, j: (0, 0)))
        args.append(w_extra)
        out_shape.append(jax.ShapeDtypeStruct((M, ne), BF16))
        out_specs.append(pl.BlockSpec((tm, ne), lambda i, j: (i, 0)))
    res = pl.pallas_call(
        functools.partial(_norm_proj_kernel, has_extra=has_extra),
        grid=(M // tm, N // tn),
        in_specs=in_specs,
        out_specs=out_specs,
        out_shape=out_shape,
        scratch_shapes=[pltpu.VMEM((tm, D), BF16)],
        compiler_params=_params(("parallel", "arbitrary")),
        name=name,
    )(*args)
    return res if has_extra else res[0]


def _proj_res_kernel(y_ref, w_ref, x_ref, o_ref):
    o_ref[...] = x_ref[...] + _dot(y_ref[...], w_ref[...])


def _proj_res(y, w, layer, x, *, tm, tn, name):
    M, K = y.shape
    N = w.shape[2]
    return pl.pallas_call(
        _proj_res_kernel,
        grid=(M // tm, N // tn),
        in_specs=[
            pl.BlockSpec((tm, K), lambda i, j: (i, 0)),
            pl.BlockSpec((None, K, tn), lambda i, j: (layer, 0, j)),
            pl.BlockSpec((tm, tn), lambda i, j: (i, j)),
        ],
        out_specs=pl.BlockSpec((tm, tn), lambda i, j: (i, j)),
        out_shape=jax.ShapeDtypeStruct((M, N), F32),
        compiler_params=_params(("parallel", "arbitrary")),
        name=name,
    )(y, w, x)


def _interleave(*lists):
    tagged = []
    for li, steps in enumerate(lists):
        tagged += [((k + 0.5) / len(steps), li, f) for k, f in enumerate(steps)]
    tagged.sort(key=lambda t: t[:2])
    return [f for _, _, f in tagged]


def _ffn_kernel(x_ref, nw_ref, wg_ref, wv_ref, cwg_ref, cwv_ref, cbg_ref, cbv_ref, wd_ref, *rest,
                tiles_per_seq, final_norm, cast_next, sub):
    rest = list(rest)
    fnw_ref = rest.pop(0) if final_norm else None
    if cast_next:
        wun_ref, wdn_ref = rest.pop(0), rest.pop(0)
        o_ref, wun_o_ref, wdn_o_ref = rest.pop(0), rest.pop(0), rest.pop(0)
        wun_o_ref[...] = wun_ref[...].astype(BF16)
        wdn_o_ref[...] = wdn_ref[...].astype(BF16)
    else:
        o_ref = rest.pop(0)
    h_ref, carg_ref, carv_ref, ug_ref, uv_ref, act_ref = rest
    i = pl.program_id(0)
    j = pl.program_id(1)
    nj = pl.num_programs(1)
    tm, dm = x_ref.shape
    tf = wd_ref.shape[0]
    nsub = tf // sub
    pad = SUBLANES

    @pl.when(j == 0)
    def _():
        x = x_ref[...]
        ms = jnp.mean(x * x, axis=-1, keepdims=True)
        h_ref[...] = (x * lax.rsqrt(ms + RMS_EPS) * nw_ref[...]).astype(BF16)
        o_ref[...] = x

    seq_start = (i % tiles_per_seq) == 0

    def up_steps(s):
        p = s % 2
        cs = pl.ds(s * sub, sub)
        acc = {}

        def kdot(w_ref, key, k):
            def f():
                ks = pl.ds(k * MXU_TILE, MXU_TILE)
                d = _dot(h_ref[:, ks], w_ref[ks, cs])
                acc[key] = d if k == 0 else acc[key] + d
            return f

        def park(key, u_ref, car_ref):
            def f():
                u = acc[key]
                u_ref[p, 0:pad, :] = jnp.where(seq_start, 0.0, car_ref[j, :, cs])
                car_ref[j, :, cs] = u[tm - pad:, :]
                u_ref[p, pad:pad + tm, :] = u
            return f

        steps = [kdot(wg_ref, "g", k) for k in range(dm // MXU_TILE)] + [park("g", ug_ref, carg_ref)]
        steps += [kdot(wv_ref, "v", k) for k in range(dm // MXU_TILE)] + [park("v", uv_ref, carv_ref)]
        return steps

    def gate_steps(s):
        p = s % 2
        cs = pl.ds(s * sub, sub)

        def conv(u_ref, cw_ref, cb_ref, r0):
            cw = cw_ref[:, cs]
            kw = cw.shape[0]
            y = cb_ref[:, cs] + cw[kw - 1:kw, :] * u_ref[p, pad + r0:pad + r0 + ROW_BLK, :]
            for d in range(1, kw):
                y = y + cw[kw - 1 - d:kw - d, :] * u_ref[p, pad + r0 - d:pad + r0 - d + ROW_BLK, :]
            return y

        def block(r0):
            def f():
                gate = conv(ug_ref, cwg_ref, cbg_ref, r0)
                val = conv(uv_ref, cwv_ref, cbv_ref, r0)
                act_ref[p, r0:r0 + ROW_BLK, :] = (_gelu_tanh(gate) * val).astype(BF16)
            return f

        return [block(r0) for r0 in range(0, tm, ROW_BLK)]

    def down_steps(s):
        p = s % 2
        cs = pl.ds(s * sub, sub)

        def ndot(n):
            def f():
                ns = pl.ds(n * MXU_TILE, MXU_TILE)
                o_ref[:, ns] += _dot(act_ref[p], wd_ref[cs, ns])
            return f

        return [ndot(n) for n in range(dm // MXU_TILE)]

    for f in up_steps(0):
        f()
    for s in range(nsub):
        slot = _interleave(gate_steps(s),
                           up_steps(s + 1) if s + 1 < nsub else [],
                           down_steps(s - 1) if s > 0 else [])
        for f in slot:
            f()
    for f in down_steps(nsub - 1):
        f()

    if final_norm:
        @pl.when(j == nj - 1)
        def _():
            y = o_ref[...]
            ms = jnp.mean(y * y, axis=-1, keepdims=True)
            o_ref[...] = y * lax.rsqrt(ms + RMS_EPS) * fnw_ref[...]


def _ffn(x, nw, w_up, conv_w, conv_b, w_down, layer, final_nw, next_f32, *, seq_len, tm, tf, sub, name):
    M, D = x.shape
    F = w_down.shape[1]
    nj = F // tf
    ni = M // tm
    final_norm = final_nw is not None
    cast_next = next_f32 is not None
    in_specs = [
        pl.BlockSpec((tm, D), lambda i, j: (i, 0)),
        pl.BlockSpec((1, D), lambda i, j: (0, 0)),
        pl.BlockSpec((None, D, tf), lambda i, j: (layer, 0, j)),
        pl.BlockSpec((None, D, tf), lambda i, j: (layer, 0, nj + j)),
        pl.BlockSpec((conv_w.shape[0], tf), lambda i, j: (0, j)),
        pl.BlockSpec((conv_w.shape[0], tf), lambda i, j: (0, nj + j)),
        pl.BlockSpec((1, tf), lambda i, j: (0, j)),
        pl.BlockSpec((1, tf), lambda i, j: (0, nj + j)),
        pl.BlockSpec((None, tf, D), lambda i, j: (layer, j, 0)),
    ]
    cb = conv_b.reshape(1, 2 * F)
    args = [x, nw.reshape(1, D), w_up, w_up, conv_w, conv_w, cb, cb, w_down]
    if final_norm:
        in_specs.append(pl.BlockSpec((1, D), lambda i, j: (0, 0)))
        args.append(final_nw.reshape(1, D))
    out_specs = [pl.BlockSpec((tm, D), lambda i, j: (i, 0))]
    out_shape = [jax.ShapeDtypeStruct((M, D), F32)]
    if cast_next:
        nxt_up, nxt_down, nxt = next_f32
        up_blk = (None, D // ni, 2 * F // nj)
        dn_blk = (None, F // (ni * nj), D)
        in_specs += [pl.BlockSpec(up_blk, lambda i, j: (nxt, i, j)),
                     pl.BlockSpec(dn_blk, lambda i, j: (nxt, i * nj + j, 0))]
        args += [nxt_up, nxt_down]
        out_specs += [pl.BlockSpec(up_blk, lambda i, j: (0, i, j)),
                      pl.BlockSpec(dn_blk, lambda i, j: (0, i * nj + j, 0))]
        out_shape += [jax.ShapeDtypeStruct((1, D, 2 * F), BF16), jax.ShapeDtypeStruct((1, F, D), BF16)]
    res = pl.pallas_call(
        functools.partial(_ffn_kernel, tiles_per_seq=seq_len // tm, final_norm=final_norm,
                          cast_next=cast_next, sub=sub),
        grid=(ni, nj),
        in_specs=in_specs,
        out_specs=out_specs,
        out_shape=out_shape,
        scratch_shapes=[
            pltpu.VMEM((tm, D), BF16),
            pltpu.VMEM((nj, SUBLANES, tf), F32),
            pltpu.VMEM((nj, SUBLANES, tf), F32),
            pltpu.VMEM((2, SUBLANES + tm, sub), F32),
            pltpu.VMEM((2, SUBLANES + tm, sub), F32),
            pltpu.VMEM((2, tm, sub), BF16),
        ],
        compiler_params=_params(("arbitrary", "arbitrary")),
        name=name,
    )(*args)
    return res if cast_next else res[0]


def _tile_masks(tc):
    r = lax.broadcasted_iota(jnp.int32, (tc, tc), 0)
    c = lax.broadcasted_iota(jnp.int32, (tc, tc), 1)
    shift = CHUNK.bit_length() - 1
    same = lax.shift_right_logical(r, shift) == lax.shift_right_logical(c, shift)
    return r, c, same


def _gla_kernel(q_ref, k_ref, v_ref, g_ref, z_ref, wgk_ref, bgk_ref, nw_ref, o_ref, st_ref, *,
                n_chunks, q_scale):
    @pl.when(pl.program_id(1) == 0)
    def _():
        st_ref[...] = jnp.zeros_like(st_ref)

    nb, tc, dk = q_ref.shape
    r, c, same = _tile_masks(tc)
    lower = same & (r >= c)
    upper = same & (r < c)
    tri = lower.astype(BF16)
    chunk_rows = [slice(ci * CHUNK, (ci + 1) * CHUNK) for ci in range(n_chunks)]

    def chain(bi):
        t = {}

        def gate():
            pre = _dot(z_ref[bi], wgk_ref[...]) + bgk_ref[...]
            log_a = -_softplus(-pre) / GLA_GATE_TAU
            t["hi"] = log_a.astype(BF16)
            t["lo"] = (log_a - t["hi"].astype(F32)).astype(BF16)

        def cumsum():
            b = _dot(tri, t["hi"]) + _dot(tri, t["lo"])
            t["b"] = b
            t["b_last"] = [b[(ci + 1) * CHUNK - 1:(ci + 1) * CHUNK, :] for ci in range(n_chunks)]

        def decays():
            b = t["b"]
            b_last_rows = jnp.concatenate(
                [jnp.broadcast_to(bl, (CHUNK, dk)) for bl in t["b_last"]], axis=0)
            eb = jnp.exp(b)
            enb = jnp.exp(-b)
            q = q_ref[bi].astype(F32) * q_scale
            k = k_ref[bi].astype(F32)
            t["q_dec"] = (q * eb).astype(BF16)
            t["k_grow"] = (k * enb).astype(BF16)
            t["q_grow"] = (q * enb).astype(BF16)
            t["k_dec"] = (k * eb).astype(BF16)
            t["k_in"] = (k * jnp.exp(b_last_rows - b)).astype(BF16)

        def intra():
            scores = jnp.where(lower, _dot_nt(t["q_dec"], t["k_grow"]),
                               jnp.where(upper, _dot_nt(t["q_grow"], t["k_dec"]), 0.0))
            t["o_intra"] = _dot(scores.astype(BF16), v_ref[bi])

        def state():
            v = v_ref[bi]
            incs = [_dot_tn(v[rows, :], t["k_in"][rows, :]) for rows in chunk_rows]
            states = [st_ref[bi]]
            for ci in range(n_chunks):
                states.append(jnp.exp(t["b_last"][ci]) * states[ci] + incs[ci])
            st_ref[bi] = states[n_chunks]
            t["states"] = states

        def readout():
            o = jnp.concatenate(
                [t["o_intra"][rows, :] + _dot_nt(t["q_dec"][rows, :], t["states"][ci].astype(BF16))
                 for ci, rows in enumerate(chunk_rows)], axis=0)
            o = o * lax.rsqrt(jnp.mean(o * o, axis=-1, keepdims=True) + RMS_EPS) * nw_ref[...]
            g = g_ref[bi].astype(F32)
            o_ref[bi] = (o * (g * _sigmoid(g))).astype(o_ref.dtype)

        return [gate, cumsum, decays, intra, state, readout]

    for f in _interleave(*[chain(bi) for bi in range(nb)]):
        f()


def _gla_core(proj, z, wgk, bgk, norm_w, *, batch, seq_len, tc, name):
    M, n_proj = proj.shape
    kd = wgk.shape[1]
    vd = norm_w.shape[0]
    H = GLA_HEADS
    dk, dv = kd // H, vd // H
    nt = seq_len // tc
    proj3 = proj.reshape(batch, seq_len, n_proj)
    y = pl.pallas_call(
        functools.partial(_gla_kernel, n_chunks=tc // CHUNK, q_scale=dk ** -0.5),
        grid=(H, nt),
        in_specs=[
            pl.BlockSpec((batch, tc, dk), lambda h, n: (0, n, h)),
            pl.BlockSpec((batch, tc, dk), lambda h, n: (0, n, kd // dk + h)),
            pl.BlockSpec((batch, tc, dv), lambda h, n: (0, n, 2 * kd // dv + h)),
            pl.BlockSpec((batch, tc, dv), lambda h, n: (0, n, (2 * kd + vd) // dv + h)),
            pl.BlockSpec((batch, tc, z.shape[1]), lambda h, n: (0, n, 0)),
            pl.BlockSpec((wgk.shape[0], dk), lambda h, n: (0, h)),
            pl.BlockSpec((1, dk), lambda h, n: (0, h)),
            pl.BlockSpec((1, dv), lambda h, n: (0, h)),
        ],
        out_specs=pl.BlockSpec((batch, tc, dv), lambda h, n: (0, n, h)),
        out_shape=jax.ShapeDtypeStruct((batch, seq_len, vd), BF16),
        scratch_shapes=[pltpu.VMEM((batch, dv, dk), F32)],
        compiler_params=_params(("arbitrary", "arbitrary")),
        name=name,
    )(proj3, proj3, proj3, proj3, z.reshape(batch, seq_len, z.shape[1]), wgk,
      bgk.reshape(1, kd), norm_w.reshape(1, vd))
    return y.reshape(M, vd)


def _gla_layer_kernel(x_ref, nw_ref, w_ref, wz_ref, wgk_ref, bgk_ref, gnw_ref, o_ref,
                      h_ref, st_ref, z_ref, q_ref, k_ref, v_ref, g_ref, *, heads, q_scale):
    nb, tc, dm = x_ref.shape
    kd = wgk_ref.shape[1]
    vd = o_ref.shape[2]
    dk, dv = kd // heads, vd // heads
    n_chunks = tc // CHUNK

    @pl.when(pl.program_id(0) == 0)
    def _():
        st_ref[...] = jnp.zeros_like(st_ref)

    for bi in range(nb):
        x = x_ref[bi]
        ms = jnp.mean(x * x, axis=-1, keepdims=True)
        h_ref[bi * tc:(bi + 1) * tc, :] = (x * lax.rsqrt(ms + RMS_EPS) * nw_ref[...]).astype(BF16)
    z_ref[...] = _dot(h_ref[...], wz_ref[...]).astype(BF16)

    r, c, same = _tile_masks(tc)
    lower = same & (r >= c)
    upper = same & (r < c)
    tri = lower.astype(BF16)
    chunk_rows = [slice(ci * CHUNK, (ci + 1) * CHUNK) for ci in range(n_chunks)]

    def proj_steps(hd):
        p = hd % 2
        steps = []
        for dst, col0, width in ((q_ref, hd * dk, dk), (k_ref, kd + hd * dk, dk),
                                 (v_ref, 2 * kd + hd * dv, dv), (g_ref, 2 * kd + vd + hd * dv, dv)):
            for c0 in range(0, width, MXU_TILE):
                acc = {}

                def kdot(kb, col=col0 + c0, acc=acc):
                    def f():
                        ks = pl.ds(kb * MXU_TILE, MXU_TILE)
                        d = _dot(h_ref[:, ks], w_ref[ks, pl.ds(col, MXU_TILE)])
                        acc["v"] = d if kb == 0 else acc["v"] + d
                    return f

                def park(dst=dst, c0=c0, acc=acc):
                    def f():
                        dst[p, :, c0:c0 + MXU_TILE] = acc["v"].astype(dst.dtype)
                    return f

                steps += [kdot(kb) for kb in range(dm // MXU_TILE)] + [park()]
        return steps

    def chain(hd, bi):
        p = hd % 2
        rows_b = slice(bi * tc, (bi + 1) * tc)
        kcols = slice(hd * dk, (hd + 1) * dk)
        vcols = slice(hd * dv, (hd + 1) * dv)
        t = {}

        def gate():
            pre = _dot(z_ref[rows_b, :], wgk_ref[:, kcols]) + bgk_ref[:, kcols]
            log_a = -_softplus(-pre) / GLA_GATE_TAU
            t["hi"] = log_a.astype(BF16)
            t["lo"] = (log_a - t["hi"].astype(F32)).astype(BF16)

        def cumsum():
            b = _dot(tri, t["hi"]) + _dot(tri, t["lo"])
            t["b"] = b
            t["b_last"] = [b[(ci + 1) * CHUNK - 1:(ci + 1) * CHUNK, :] for ci in range(n_chunks)]

        def decays():
            b = t["b"]
            b_last_rows = jnp.concatenate(
                [jnp.broadcast_to(bl, (CHUNK, dk)) for bl in t["b_last"]], axis=0)
            eb = jnp.exp(b)
            enb = jnp.exp(-b)
            q = q_ref[p, rows_b, :] * q_scale
            k = k_ref[p, rows_b, :]
            t["q_dec"] = (q * eb).astype(BF16)
            t["k_grow"] = (k * enb).astype(BF16)
            t["q_grow"] = (q * enb).astype(BF16)
            t["k_dec"] = (k * eb).astype(BF16)
            t["k_in"] = (k * jnp.exp(b_last_rows - b)).astype(BF16)

        def intra():
            scores = jnp.where(lower, _dot_nt(t["q_dec"], t["k_grow"]),
                               jnp.where(upper, _dot_nt(t["q_grow"], t["k_dec"]), 0.0))
            t["o_intra"] = _dot(scores.astype(BF16), v_ref[p, rows_b, :])

        def state():
            v = v_ref[p, rows_b, :]
            incs = [_dot_tn(v[rows, :], t["k_in"][rows, :]) for rows in chunk_rows]
            states = [st_ref[bi, hd]]
            for ci in range(n_chunks):
                states.append(jnp.exp(t["b_last"][ci]) * states[ci] + incs[ci])
            st_ref[bi, hd] = states[n_chunks]
            t["states"] = states

        def readout():
            o = jnp.concatenate(
                [t["o_intra"][rows, :] + _dot_nt(t["q_dec"][rows, :], t["states"][ci].astype(BF16))
                 for ci, rows in enumerate(chunk_rows)], axis=0)
            o = o * lax.rsqrt(jnp.mean(o * o, axis=-1, keepdims=True) + RMS_EPS) * gnw_ref[:, vcols]
            g = g_ref[p, rows_b, :]
            o_ref[bi, :, vcols] = (o * (g * _sigmoid(g))).astype(o_ref.dtype)

        return [gate, cumsum, decays, intra, state, readout]

    for f in proj_steps(0):
        f()
    for hd in range(heads):
        slot = [chain(hd, bi) for bi in range(nb)]
        if hd + 1 < heads:
            slot.append(proj_steps(hd + 1))
        for f in _interleave(*slot):
            f()


def _gla_layer(x3, nw, w_in, layer, wgk, bgk, gnorm_w, *, tc, name):
    B, S, D = x3.shape
    kd = wgk.shape[1]
    vd = gnorm_w.shape[0]
    H = GLA_HEADS
    dk, dv = kd // H, vd // H
    rows = B * tc
    const = lambda n: (0, 0)
    return pl.pallas_call(
        functools.partial(_gla_layer_kernel, heads=H, q_scale=dk ** -0.5),
        grid=(S // tc,),
        in_specs=[
            pl.BlockSpec((B, tc, D), lambda n: (0, n, 0)),
            pl.BlockSpec((1, D), const),
            pl.BlockSpec((None, D, 2 * kd + 2 * vd), lambda n: (layer, 0, 0),
                         pipeline_mode=pl.Buffered(1)),
            pl.BlockSpec((None, D, LANES), lambda n: (layer, 0, (2 * kd + 2 * vd) // LANES)),
            pl.BlockSpec(wgk.shape, const),
            pl.BlockSpec((1, kd), const),
            pl.BlockSpec((1, vd), const),
        ],
        out_specs=pl.BlockSpec((B, tc, vd), lambda n: (0, n, 0)),
        out_shape=jax.ShapeDtypeStruct((B, S, vd), BF16),
        scratch_shapes=[
            pltpu.VMEM((rows, D), BF16),
            pltpu.VMEM((B, H, dv, dk), F32),
            pltpu.VMEM((rows, LANES), BF16),
            pltpu.VMEM((2, rows, dk), F32),
            pltpu.VMEM((2, rows, dk), F32),
            pltpu.VMEM((2, rows, dv), BF16),
            pltpu.VMEM((2, rows, dv), F32),
        ],
        compiler_params=_params(("arbitrary",)),
        name=name,
    )(x3, nw.reshape(1, D), w_in, w_in, wgk, bgk.reshape(1, kd), gnorm_w.reshape(1, vd))


def _rope_kernel(cos_ref, sin_ref):
    tt, half = cos_ref.shape
    pos = (pl.program_id(0) * tt + lax.broadcasted_iota(jnp.int32, (tt, half), 0)).astype(F32)
    jj = lax.broadcasted_iota(jnp.int32, (tt, half), 1).astype(F32)
    inv = jnp.exp(-(jj / half) * math.log(ROPE_BASE))
    ang = pos * inv
    cos_ref[...] = jnp.cos(ang)
    sin_ref[...] = jnp.sin(ang)


def _rope_tables(seq_len, half, *, tt):
    return pl.pallas_call(
        _rope_kernel,
        grid=(seq_len // tt,),
        out_specs=[pl.BlockSpec((tt, half), lambda i: (i, 0))] * 2,
        out_shape=[jax.ShapeDtypeStruct((seq_len, half), F32)] * 2,
        compiler_params=_params(("parallel",)),
        name="rope_tables",
    )()


def _ret_kernel(q_ref, k_ref, v_ref, g_ref, cos_ref, sin_ref, nw_ref, o_ref,
                st_ref, dmat_ref, xi_ref, zeta_ref, *, k_scale):
    nb, tc, dk = q_ref.shape
    half = dk // 2
    head = pl.program_id(0).astype(F32)

    def log_gamma(shape):
        return jnp.log(1.0 - jnp.exp2(-5.0 - jnp.full(shape, head, F32)))

    @pl.when(pl.program_id(1) == 0)
    def _():
        st_ref[...] = jnp.zeros_like(st_ref)
        r, c, _ = _tile_masks(tc)
        shift = CHUNK.bit_length() - 1
        reads = lax.shift_right_logical(c, shift) <= lax.shift_right_logical(r, shift)
        dist = jnp.abs(r - c).astype(F32)
        dmat_ref[...] = jnp.where(reads, jnp.exp(log_gamma((tc, tc)) * dist), 0.0)
        pos = lax.broadcasted_iota(jnp.int32, (tc, dk), 0).astype(F32)
        lg = log_gamma((tc, dk))
        xi_ref[...] = jnp.exp(lg * (pos + 1.0))
        zeta_ref[...] = jnp.exp(lg * (tc - 1.0 - pos))

    decay = jnp.exp(log_gamma((1, dk)) * float(tc))

    def rope(x):
        x1, x2 = x[:, :half], x[:, half:]
        cos, sin = cos_ref[...], sin_ref[...]
        return jnp.concatenate([x1 * cos - x2 * sin, x2 * cos + x1 * sin], axis=-1)

    def chain(bi):
        t = {}

        def rotate():
            t["q"] = rope(q_ref[bi].astype(F32))
            t["k"] = rope(k_ref[bi].astype(F32)) * k_scale

        def scores():
            t["s"] = (_dot_nt(t["q"].astype(BF16), t["k"].astype(BF16)) * dmat_ref[...]).astype(BF16)

        def mix():
            st = st_ref[bi]
            v = v_ref[bi]
            t["o"] = _dot(t["s"], v) + _dot_nt((t["q"] * xi_ref[...]).astype(BF16), st.astype(BF16))
            st_ref[bi] = decay * st + _dot_tn(v, (t["k"] * zeta_ref[...]).astype(BF16))

        def finish():
            o = t["o"]
            oc = o - jnp.mean(o, axis=-1, keepdims=True)
            o = oc * lax.rsqrt(jnp.mean(oc * oc, axis=-1, keepdims=True) + RMS_EPS) * nw_ref[...]
            g = g_ref[bi].astype(F32)
            o_ref[bi] = (o * (g * _sigmoid(g))).astype(o_ref.dtype)

        return [rotate, scores, mix, finish]

    for f in _interleave(*[chain(bi) for bi in range(nb)]):
        f()


def _ret_core(proj, cos, sin, norm_w, *, batch, seq_len, tc, name):
    M, n_proj = proj.shape
    vd = norm_w.shape[0]
    H = RET_HEADS
    dv = vd // H
    dk = cos.shape[1] * 2
    kd = H * dk
    proj3 = proj.reshape(batch, seq_len, n_proj)
    y = pl.pallas_call(
        functools.partial(_ret_kernel, k_scale=dk ** -0.5),
        grid=(H, seq_len // tc),
        in_specs=[
            pl.BlockSpec((batch, tc, dk), lambda h, n: (0, n, h)),
            pl.BlockSpec((batch, tc, dk), lambda h, n: (0, n, H + h)),
            pl.BlockSpec((batch, tc, dv), lambda h, n: (0, n, 2 * kd // dv + h)),
            pl.BlockSpec((batch, tc, dv), lambda h, n: (0, n, (2 * kd + vd) // dv + h)),
            pl.BlockSpec((tc, dk // 2), lambda h, n: (n, 0)),
            pl.BlockSpec((tc, dk // 2), lambda h, n: (n, 0)),
            pl.BlockSpec((1, dv), lambda h, n: (0, h)),
        ],
        out_specs=pl.BlockSpec((batch, tc, dv), lambda h, n: (0, n, h)),
        out_shape=jax.ShapeDtypeStruct((batch, seq_len, vd), BF16),
        scratch_shapes=[
            pltpu.VMEM((batch, dv, dk), F32),
            pltpu.VMEM((tc, tc), F32),
            pltpu.VMEM((tc, dk), F32),
            pltpu.VMEM((tc, dk), F32),
        ],
        compiler_params=_params(("arbitrary", "arbitrary")),
        name=name,
    )(proj3, proj3, proj3, proj3, cos, sin, norm_w.reshape(1, vd))
    return y.reshape(M, vd)


def _lru_kernel(xb_ref, yb_ref, cw_ref, cb_ref, wga_ref, bga_ref, wgx_ref, bgx_ref, lam_ref, o_ref,
                hcar_ref, ccar_ref, a_ref, u_ref, hs_ref, *, n_blocks):
    tt, width = xb_ref.shape
    kw = cw_ref.shape[0]

    @pl.when(pl.program_id(1) == 0)
    def _():
        hcar_ref[...] = jnp.zeros_like(hcar_ref)
        ccar_ref[...] = jnp.zeros_like(ccar_ref)

    bw = width // n_blocks
    for nb in range(n_blocks):
        cs = pl.ds(nb * bw, bw)
        xb = xb_ref[:, cs].astype(F32)
        ext = jnp.concatenate([ccar_ref[:, cs], xb], axis=0)
        ccar_ref[:, cs] = xb[tt - SUBLANES:, :]
        cw = cw_ref[:, cs]
        xc = cw[kw - 1:kw, :] * xb + cb_ref[:, cs]
        for d in range(1, kw):
            xc = xc + cw[kw - 1 - d:kw - d, :] * ext[SUBLANES - d:SUBLANES - d + tt, :]
        xcb = xc.astype(BF16)
        rg = _sigmoid(_dot(xcb, wga_ref[nb]) + bga_ref[:, cs])
        ig = _sigmoid(_dot(xcb, wgx_ref[nb]) + bgx_ref[:, cs])
        a = jnp.exp(rg * (-LRU_C * _softplus(-lam_ref[:, cs])))
        a_ref[:, cs] = a
        u_ref[:, cs] = xc * ig * jnp.sqrt(1.0 - a * a)

    def step(t, h):
        h = a_ref[pl.ds(t, 1), :] * h + u_ref[pl.ds(t, 1), :]
        hs_ref[pl.ds(t, 1), :] = h
        return h

    hcar_ref[0:1, :] = lax.fori_loop(0, tt, step, hcar_ref[0:1, :], unroll=8)

    for nb in range(n_blocks):
        cs = pl.ds(nb * bw, bw)
        o_ref[:, cs] = (hs_ref[:, cs] * _gelu_tanh(yb_ref[:, cs].astype(F32))).astype(o_ref.dtype)


def _lru_core(proj, conv_w, conv_b, w_ga, b_ga, w_gx, b_gx, lam, layer, *, batch, seq_len, tt, name):
    M = proj.shape[0]
    width = conv_w.shape[1]
    nt = seq_len // tt
    nb = w_ga.shape[1]
    vec = lambda: pl.BlockSpec((1, width), lambda b, n: (0, 0))
    wblk = lambda: pl.BlockSpec((None,) + w_ga.shape[1:], lambda b, n: (layer, 0, 0, 0))
    return pl.pallas_call(
        functools.partial(_lru_kernel, n_blocks=nb),
        grid=(batch, nt),
        in_specs=[
            pl.BlockSpec((tt, width), lambda b, n: (b * nt + n, 0)),
            pl.BlockSpec((tt, width), lambda b, n: (b * nt + n, 1)),
            pl.BlockSpec(conv_w.shape, lambda b, n: (0, 0)),
            vec(), wblk(), vec(), wblk(), vec(), vec(),
        ],
        out_specs=pl.BlockSpec((tt, width), lambda b, n: (b * nt + n, 0)),
        out_shape=jax.ShapeDtypeStruct((M, width), BF16),
        scratch_shapes=[
            pltpu.VMEM((SUBLANES, width), F32),
            pltpu.VMEM((SUBLANES, width), F32),
            pltpu.VMEM((tt, width), F32),
            pltpu.VMEM((tt, width), F32),
            pltpu.VMEM((tt, width), F32),
        ],
        compiler_params=_params(("arbitrary", "arbitrary")),
        name=name,
    )(proj, proj, conv_w, conv_b.reshape(1, width), w_ga, b_ga.reshape(1, width),
      w_gx, b_gx.reshape(1, width), lam.reshape(1, width))


def _lru_layer_kernel(x_ref, nw_ref, w_ref, cw_ref, cb_ref, wga_ref, bga_ref, wgx_ref, bgx_ref,
                      lam_ref, o_ref, h_ref, hcar_ref, ccar_ref, xb_ref, yb_ref, a_ref, u_ref, hs_ref,
                      gy_ref, *, n_blocks):
    tt, dm = x_ref.shape
    width = o_ref.shape[1]
    kw = cw_ref.shape[0]
    bw = width // n_blocks

    @pl.when(pl.program_id(1) == 0)
    def _():
        hcar_ref[...] = jnp.zeros_like(hcar_ref)
        ccar_ref[...] = jnp.zeros_like(ccar_ref)

    x = x_ref[...]
    ms = jnp.mean(x * x, axis=-1, keepdims=True)
    h_ref[...] = (x * lax.rsqrt(ms + RMS_EPS) * nw_ref[...]).astype(BF16)

    def proj_steps(nb):
        p = nb % 2
        steps = []
        for dst, col in ((xb_ref, nb * bw), (yb_ref, width + nb * bw)):
            acc = {}

            def kdot(kb, col=col, acc=acc):
                def f():
                    ks = pl.ds(kb * MXU_TILE, MXU_TILE)
                    d = _dot(h_ref[:, ks], w_ref[ks, pl.ds(col, bw)])
                    acc["v"] = d if kb == 0 else acc["v"] + d
                return f

            def park(dst=dst, acc=acc):
                def f():
                    dst[p] = acc["v"]
                return f

            steps += [kdot(kb) for kb in range(dm // MXU_TILE)] + [park()]
        return steps

    def gate_steps(nb):
        p = nb % 2
        cs = pl.ds(nb * bw, bw)
        t = {}

        def conv():
            xb = xb_ref[p]
            ext = jnp.concatenate([ccar_ref[:, cs], xb], axis=0)
            ccar_ref[:, cs] = xb[tt - SUBLANES:, :]
            cw = cw_ref[:, cs]
            xc = cw[kw - 1:kw, :] * xb + cb_ref[:, cs]
            for d in range(1, kw):
                xc = xc + cw[kw - 1 - d:kw - d, :] * ext[SUBLANES - d:SUBLANES - d + tt, :]
            t["xc"] = xc

        def gates():
            xc = t["xc"]
            xcb = xc.astype(BF16)
            rg = _sigmoid(_dot(xcb, wga_ref[nb]) + bga_ref[:, cs])
            ig = _sigmoid(_dot(xcb, wgx_ref[nb]) + bgx_ref[:, cs])
            a = jnp.exp(rg * (-LRU_C * _softplus(-lam_ref[:, cs])))
            a_ref[:, cs] = a
            y = 1.0 - a * a
            u_ref[:, cs] = xc * ig * jnp.where(y > 0.0, y * lax.rsqrt(y), 0.0)

        def out_gate():
            gy_ref[:, cs] = _gelu_tanh(yb_ref[p])

        return [conv, gates, out_gate]

    for f in proj_steps(0):
        f()
    for nb in range(n_blocks):
        slot = [gate_steps(nb)] + ([proj_steps(nb + 1)] if nb + 1 < n_blocks else [])
        for f in _interleave(*slot):
            f()

    def step(t, h):
        h = a_ref[pl.ds(t, 1), :] * h + u_ref[pl.ds(t, 1), :]
        hs_ref[pl.ds(t, 1), :] = h
        return h

    hcar_ref[0:1, :] = lax.fori_loop(0, tt, step, hcar_ref[0:1, :], unroll=8)

    for nb in range(n_blocks):
        cs = pl.ds(nb * bw, bw)
        o_ref[:, cs] = (hs_ref[:, cs] * gy_ref[:, cs]).astype(o_ref.dtype)


def _lru_layer(x, nw, w_in, conv_w, conv_b, w_ga, b_ga, w_gx, b_gx, lam, layer, *, batch, seq_len, tt,
               name):
    M, D = x.shape
    width = conv_w.shape[1]
    nt = seq_len // tt
    nb = w_ga.shape[1]
    bw = width // nb
    vec = lambda: pl.BlockSpec((1, width), lambda b, n: (0, 0))
    wblk = lambda: pl.BlockSpec((None,) + w_ga.shape[1:], lambda b, n: (layer, 0, 0, 0))
    return pl.pallas_call(
        functools.partial(_lru_layer_kernel, n_blocks=nb),
        grid=(batch, nt),
        in_specs=[
            pl.BlockSpec((tt, D), lambda b, n: (b * nt + n, 0)),
            pl.BlockSpec((1, D), lambda b, n: (0, 0)),
            pl.BlockSpec((None, D, 2 * width), lambda b, n: (layer, 0, 0),
                         pipeline_mode=pl.Buffered(1)),
            pl.BlockSpec(conv_w.shape, lambda b, n: (0, 0)),
            vec(), wblk(), vec(), wblk(), vec(), vec(),
        ],
        out_specs=pl.BlockSpec((tt, width), lambda b, n: (b * nt + n, 0)),
        out_shape=jax.ShapeDtypeStruct((M, width), BF16),
        scratch_shapes=[
            pltpu.VMEM((tt, D), BF16),
            pltpu.VMEM((SUBLANES, width), F32),
            pltpu.VMEM((SUBLANES, width), F32),
            pltpu.VMEM((2, tt, bw), F32),
            pltpu.VMEM((2, tt, bw), F32),
            pltpu.VMEM((tt, width), F32),
            pltpu.VMEM((tt, width), F32),
            pltpu.VMEM((tt, width), F32),
            pltpu.VMEM((tt, width), F32),
        ],
        compiler_params=_params(("arbitrary", "arbitrary")),
        name=name,
    )(x, nw.reshape(1, D), w_in, conv_w, conv_b.reshape(1, width), w_ga, b_ga.reshape(1, width),
      w_gx, b_gx.reshape(1, width), lam.reshape(1, width))


def kernel(x, norm_mix_w, norm_ffn_w, norm_out_w, gla_w_in, gla_w_gk, gla_b_gk, gla_norm_w, gla_w_out, lru_w_in, lru_conv_w, lru_conv_b, lru_w_ga, lru_b_ga, lru_w_gx, lru_b_gx, lru_lambda, lru_w_out, ret_w_in, ret_norm_w, ret_w_out, ffn_w_up, ffn_conv_w, ffn_conv_b, ffn_w_down):
    B, S, D = x.shape
    M = B * S
    depth = norm_mix_w.shape[0]
    xf = x.reshape(M, D)

    gla_kd = gla_w_gk.shape[2]
    gla_vd = gla_norm_w.shape[1]
    gla_main = 2 * gla_kd + 2 * gla_vd
    rank = gla_w_gk.shape[1]
    ret_dk = ret_w_in.shape[2] // (6 * RET_HEADS)
    cos = sin = None
    tm_p = min(1024, M)
    tm_f = min(512, S)
    tc = min(256, S)
    tt_lru = min(512, S)
    tc_ret = min(512, S)

    gla_w_in_b = jnp.pad(gla_w_in, ((0, 0), (0, 0), (0, LANES - rank))).astype(BF16)
    gla_w_out_b = gla_w_out.astype(BF16)
    lru_w_in_b, lru_w_out_b = lru_w_in.astype(BF16), lru_w_out.astype(BF16)
    lru_w_ga_b, lru_w_gx_b = lru_w_ga.astype(BF16), lru_w_gx.astype(BF16)
    ret_w_in_b, ret_w_out_b = ret_w_in.astype(BF16), ret_w_out.astype(BF16)
    ffn_w_up_b, ffn_w_down_b = ffn_w_up[:1].astype(BF16), ffn_w_down[:1].astype(BF16)

    for i in range(depth):
        kind, l = i % N_MIXERS, i // N_MIXERS
        if kind == 0:
            w_gk = jnp.pad(gla_w_gk[l].astype(BF16), ((0, LANES - rank), (0, 0)))
            y = _gla_layer(xf.reshape(B, S, D), norm_mix_w[i], gla_w_in_b, l, w_gk, gla_b_gk[l],
                           gla_norm_w[l], tc=tc, name=f"gla_mix_{i}")
            xf = _proj_res(y.reshape(M, gla_vd), gla_w_out_b, l, xf, tm=tm_p, tn=1024,
                           name=f"gla_out_{i}")
        elif kind == 1:
            y = _lru_layer(xf, norm_mix_w[i], lru_w_in_b, lru_conv_w[l], lru_conv_b[l], lru_w_ga_b,
                           lru_b_ga[l], lru_w_gx_b, lru_b_gx[l], lru_lambda[l], l,
                           batch=B, seq_len=S, tt=tt_lru, name=f"lru_mix_{i}")
            xf = _proj_res(y, lru_w_out_b, l, xf, tm=tm_p, tn=1024, name=f"lru_out_{i}")
        else:
            if cos is None:
                cos, sin = _rope_tables(S, ret_dk // 2, tt=tm_f)
            proj = _norm_proj(xf, norm_mix_w[i], ret_w_in_b, l, tm=tm_p, tn=1024, name=f"ret_in_{i}")
            y = _ret_core(proj, cos, sin, ret_norm_w[l],
                          batch=B, seq_len=S, tc=tc_ret, name=f"ret_core_{i}")
            xf = _proj_res(y, ret_w_out_b, l, xf, tm=tm_p, tn=512, name=f"ret_out_{i}")
        last = i == depth - 1
        res = _ffn(xf, norm_ffn_w[i], ffn_w_up_b, ffn_conv_w[i], ffn_conv_b[i], ffn_w_down_b, 0,
                   norm_out_w if last else None, None if last else (ffn_w_up, ffn_w_down, i + 1),
                   seq_len=S, tm=tm_f, tf=1024, sub=256, name=f"ffn_{i}")
        if last:
            xf = res
        else:
            xf, ffn_w_up_b, ffn_w_down_b = res
    return xf.reshape(B, S, D)
```

```python
import functools
import math

import jax
import jax.numpy as jnp
from jax import lax
from jax.experimental import pallas as pl
from jax.experimental.pallas import tpu as pltpu

F32 = jnp.float32
BF16 = jnp.bfloat16

CHUNK = 64
RMS_EPS = 1e-6
N_MIXERS = 3
GLA_HEADS = 4
GLA_GATE_TAU = 16.0
LRU_BLOCK_W = 256
LRU_C = 8.0
RET_HEADS = 8
ROPE_BASE = 10000.0

LANES = 128
SUBLANES = 8
MXU_TILE = 256
ROW_BLK = 64
VMEM_LIMIT = 56 * 1024 * 1024


def _params(sem):
    return pltpu.CompilerParams(dimension_semantics=sem, vmem_limit_bytes=VMEM_LIMIT)


def _sigmoid(x):
    return 0.5 * jnp.tanh(0.5 * x) + 0.5


def _gelu_tanh(x):
    c = math.sqrt(2.0 / math.pi)
    return x * (0.5 * (1.0 + jnp.tanh(c * (x + 0.044715 * (x * x * x)))))


def _softplus(x):
    return jnp.maximum(x, 0.0) + jnp.log1p(jnp.exp(-jnp.abs(x)))


def _dot(a, b):
    return jnp.dot(a, b, preferred_element_type=F32)


def _dot_nt(a, b):
    return lax.dot_general(a, b, (((1,), (1,)), ((), ())), preferred_element_type=F32)


def _dot_tn(a, b):
    return lax.dot_general(a, b, (((0,), (0,)), ((), ())), preferred_element_type=F32)


def _norm_proj_kernel(x_ref, nw_ref, w_ref, *rest, has_extra):
    if has_extra:
        we_ref, o_ref, oe_ref, h_ref = rest
    else:
        o_ref, h_ref = rest

    @pl.when(pl.program_id(1) == 0)
    def _():
        x = x_ref[...]
        ms = jnp.mean(x * x, axis=-1, keepdims=True)
        h_ref[...] = (x * lax.rsqrt(ms + RMS_EPS) * nw_ref[...]).astype(BF16)
        if has_extra:
            oe_ref[...] = _dot(h_ref[...], we_ref[...]).astype(oe_ref.dtype)

    o_ref[...] = _dot(h_ref[...], w_ref[...]).astype(o_ref.dtype)


def _norm_proj(x, nw, w, layer, w_extra=None, *, n_out=None, tm, tn, name):
    M, D = x.shape
    N = w.shape[2] if n_out is None else n_out
    has_extra = w_extra is not None
    in_specs = [
        pl.BlockSpec((tm, D), lambda i, j: (i, 0)),
        pl.BlockSpec((1, D), lambda i, j: (0, 0)),
        pl.BlockSpec((None, D, tn), lambda i, j: (layer, 0, j)),
    ]
    args = [x, nw.reshape(1, D), w]
    out_shape = [jax.ShapeDtypeStruct((M, N), BF16)]
    out_specs = [pl.BlockSpec((tm, tn), lambda i, j: (i, j))]
    if has_extra:
        ne = w_extra.shape[1]
        in_specs.append(pl.BlockSpec((D, ne), lambda i, j: (0, 0)))
        args.append(w_extra)
        out_shape.append(jax.ShapeDtypeStruct((M, ne), BF16))
        out_specs.append(pl.BlockSpec((tm, ne), lambda i, j: (i, 0)))
    res = pl.pallas_call(
        functools.partial(_norm_proj_kernel, has_extra=has_extra),
        grid=(M // tm, N // tn),
        in_specs=in_specs,
        out_specs=out_specs,
        out_shape=out_shape,
        scratch_shapes=[pltpu.VMEM((tm, D), BF16)],
        compiler_params=_params(("parallel", "arbitrary")),
        name=name,
    )(*args)
    return res if has_extra else res[0]


def _proj_res_kernel(y_ref, w_ref, x_ref, o_ref):
    o_ref[...] = x_ref[...] + _dot(y_ref[...], w_ref[...])


def _proj_res(y, w, layer, x, *, tm, tn, name):
    M, K = y.shape
    N = w.shape[2]
    return pl.pallas_call(
        _proj_res_kernel,
        grid=(M // tm, N // tn),
        in_specs=[
            pl.BlockSpec((tm, K), lambda i, j: (i, 0)),
            pl.BlockSpec((None, K, tn), lambda i, j: (layer, 0, j)),
            pl.BlockSpec((tm, tn), lambda i, j: (i, j)),
        ],
        out_specs=pl.BlockSpec((tm, tn), lambda i, j: (i, j)),
        out_shape=jax.ShapeDtypeStruct((M, N), F32),
        compiler_params=_params(("parallel", "arbitrary")),
        name=name,
    )(y, w, x)


def _interleave(*lists):
    tagged = []
    for li, steps in enumerate(lists):
        tagged += [((k + 0.5) / len(steps), li, f) for k, f in enumerate(steps)]
    tagged.sort(key=lambda t: t[:2])
    return [f for _, _, f in tagged]


def _ffn_kernel(x_ref, nw_ref, wg_ref, wv_ref, cwg_ref, cwv_ref, cbg_ref, cbv_ref, wd_ref, *rest,
                tiles_per_seq, final_norm, cast_next, sub):
    rest = list(rest)
    fnw_ref = rest.pop(0) if final_norm else None
    if cast_next:
        wun_ref, wdn_ref = rest.pop(0), rest.pop(0)
        o_ref, wun_o_ref, wdn_o_ref = rest.pop(0), rest.pop(0), rest.pop(0)
        wun_o_ref[...] = wun_ref[...].astype(BF16)
        wdn_o_ref[...] = wdn_ref[...].astype(BF16)
    else:
        o_ref = rest.pop(0)
    h_ref, carg_ref, carv_ref, ug_ref, uv_ref, act_ref = rest
    i = pl.program_id(0)
    j = pl.program_id(1)
    nj = pl.num_programs(1)
    tm, dm = x_ref.shape
    tf = wd_ref.shape[0]
    nsub = tf // sub
    pad = SUBLANES

    @pl.when(j == 0)
    def _():
        x = x_ref[...]
        ms = jnp.mean(x * x, axis=-1, keepdims=True)
        h_ref[...] = (x * lax.rsqrt(ms + RMS_EPS) * nw_ref[...]).astype(BF16)
        o_ref[...] = x

    seq_start = (i % tiles_per_seq) == 0

    def up_steps(s):
        p = s % 2
        cs = pl.ds(s * sub, sub)
        acc = {}

        def kdot(w_ref, key, k):
            def f():
                ks = pl.ds(k * MXU_TILE, MXU_TILE)
                d = _dot(h_ref[:, ks], w_ref[ks, cs])
                acc[key] = d if k == 0 else acc[key] + d
            return f

        def park(key, u_ref, car_ref):
            def f():
                u = acc[key]
                u_ref[p, 0:pad, :] = jnp.where(seq_start, 0.0, car_ref[j, :, cs])
                car_ref[j, :, cs] = u[tm - pad:, :]
                u_ref[p, pad:pad + tm, :] = u
            return f

        steps = [kdot(wg_ref, "g", k) for k in range(dm // MXU_TILE)] + [park("g", ug_ref, carg_ref)]
        steps += [kdot(wv_ref, "v", k) for k in range(dm // MXU_TILE)] + [park("v", uv_ref, carv_ref)]
        return steps

    def gate_steps(s):
        p = s % 2
        cs = pl.ds(s * sub, sub)

        def conv(u_ref, cw_ref, cb_ref, r0):
            cw = cw_ref[:, cs]
            kw = cw.shape[0]
            y = cb_ref[:, cs] + cw[kw - 1:kw, :] * u_ref[p, pad + r0:pad + r0 + ROW_BLK, :]
            for d in range(1, kw):
                y = y + cw[kw - 1 - d:kw - d, :] * u_ref[p, pad + r0 - d:pad + r0 - d + ROW_BLK, :]
            return y

        def block(r0):
            def f():
                gate = conv(ug_ref, cwg_ref, cbg_ref, r0)
                val = conv(uv_ref, cwv_ref, cbv_ref, r0)
                act_ref[p, r0:r0 + ROW_BLK, :] = (_gelu_tanh(gate) * val).astype(BF16)
            return f

        return [block(r0) for r0 in range(0, tm, ROW_BLK)]

    def down_steps(s):
        p = s % 2
        cs = pl.ds(s * sub, sub)

        def ndot(n):
            def f():
                ns = pl.ds(n * MXU_TILE, MXU_TILE)
                o_ref[:, ns] += _dot(act_ref[p], wd_ref[cs, ns])
            return f

        return [ndot(n) for n in range(dm // MXU_TILE)]

    for f in up_steps(0):
        f()
    for s in range(nsub):
        slot = _interleave(gate_steps(s),
                           up_steps(s + 1) if s + 1 < nsub else [],
                           down_steps(s - 1) if s > 0 else [])
        for f in slot:
            f()
    for f in down_steps(nsub - 1):
        f()

    if final_norm:
        @pl.when(j == nj - 1)
        def _():
            y = o_ref[...]
            ms = jnp.mean(y * y, axis=-1, keepdims=True)
            o_ref[...] = y * lax.rsqrt(ms + RMS_EPS) * fnw_ref[...]


def _ffn(x, nw, w_up, conv_w, conv_b, w_down, layer, final_nw, next_f32, *, seq_len, tm, tf, sub, name):
    M, D = x.shape
    F = w_down.shape[1]
    nj = F // tf
    ni = M // tm
    final_norm = final_nw is not None
    cast_next = next_f32 is not None
    in_specs = [
        pl.BlockSpec((tm, D), lambda i, j: (i, 0)),
        pl.BlockSpec((1, D), lambda i, j: (0, 0)),
        pl.BlockSpec((None, D, tf), lambda i, j: (layer, 0, j)),
        pl.BlockSpec((None, D, tf), lambda i, j: (layer, 0, nj + j)),
        pl.BlockSpec((conv_w.shape[0], tf), lambda i, j: (0, j)),
        pl.BlockSpec((conv_w.shape[0], tf), lambda i, j: (0, nj + j)),
        pl.BlockSpec((1, tf), lambda i, j: (0, j)),
        pl.BlockSpec((1, tf), lambda i, j: (0, nj + j)),
        pl.BlockSpec((None, tf, D), lambda i, j: (layer, j, 0)),
    ]
    cb = conv_b.reshape(1, 2 * F)
    args = [x, nw.reshape(1, D), w_up, w_up, conv_w, conv_w, cb, cb, w_down]
    if final_norm:
        in_specs.append(pl.BlockSpec((1, D), lambda i, j: (0, 0)))
        args.append(final_nw.reshape(1, D))
    out_specs = [pl.BlockSpec((tm, D), lambda i, j: (i, 0))]
    out_shape = [jax.ShapeDtypeStruct((M, D), F32)]
    if cast_next:
        nxt_up, nxt_down, nxt = next_f32
        up_blk = (None, D // ni, 2 * F // nj)
        dn_blk = (None, F // (ni * nj), D)
        in_specs += [pl.BlockSpec(up_blk, lambda i, j: (nxt, i, j)),
                     pl.BlockSpec(dn_blk, lambda i, j: (nxt, i * nj + j, 0))]
        args += [nxt_up, nxt_down]
        out_specs += [pl.BlockSpec(up_blk, lambda i, j: (0, i, j)),
                      pl.BlockSpec(dn_blk, lambda i, j: (0, i * nj + j, 0))]
        out_shape += [jax.ShapeDtypeStruct((1, D, 2 * F), BF16), jax.ShapeDtypeStruct((1, F, D), BF16)]
    res = pl.pallas_call(
        functools.partial(_ffn_kernel, tiles_per_seq=seq_len // tm, final_norm=final_norm,
                          cast_next=cast_next, sub=sub),
        grid=(ni, nj),
        in_specs=in_specs,
        out_specs=out_specs,
        out_shape=out_shape,
        scratch_shapes=[
            pltpu.VMEM((tm, D), BF16),
            pltpu.VMEM((nj, SUBLANES, tf), F32),
            pltpu.VMEM((nj, SUBLANES, tf), F32),
            pltpu.VMEM((2, SUBLANES + tm, sub), F32),
            pltpu.VMEM((2, SUBLANES + tm, sub), F32),
            pltpu.VMEM((2, tm, sub), BF16),
        ],
        compiler_params=_params(("arbitrary", "arbitrary")),
        name=name,
    )(*args)
    return res if cast_next else res[0]


def _tile_masks(tc):
    r = lax.broadcasted_iota(jnp.int32, (tc, tc), 0)
    c = lax.broadcasted_iota(jnp.int32, (tc, tc), 1)
    shift = CHUNK.bit_length() - 1
    same = lax.shift_right_logical(r, shift) == lax.shift_right_logical(c, shift)
    return r, c, same


def _gla_kernel(q_ref, k_ref, v_ref, g_ref, z_ref, wgk_ref, bgk_ref, nw_ref, o_ref, st_ref, *,
                n_chunks, q_scale):
    @pl.when(pl.program_id(1) == 0)
    def _():
        st_ref[...] = jnp.zeros_like(st_ref)

    nb, tc, dk = q_ref.shape
    r, c, same = _tile_masks(tc)
    lower = same & (r >= c)
    upper = same & (r < c)
    tri = lower.astype(BF16)
    chunk_rows = [slice(ci * CHUNK, (ci + 1) * CHUNK) for ci in range(n_chunks)]

    def chain(bi):
        t = {}

        def gate():
            pre = _dot(z_ref[bi], wgk_ref[...]) + bgk_ref[...]
            log_a = -_softplus(-pre) / GLA_GATE_TAU
            t["hi"] = log_a.astype(BF16)
            t["lo"] = (log_a - t["hi"].astype(F32)).astype(BF16)

        def cumsum():
            b = _dot(tri, t["hi"]) + _dot(tri, t["lo"])
            t["b"] = b
            t["b_last"] = [b[(ci + 1) * CHUNK - 1:(ci + 1) * CHUNK, :] for ci in range(n_chunks)]

        def decays():
            b = t["b"]
            b_last_rows = jnp.concatenate(
                [jnp.broadcast_to(bl, (CHUNK, dk)) for bl in t["b_last"]], axis=0)
            eb = jnp.exp(b)
            enb = jnp.exp(-b)
            q = q_ref[bi].astype(F32) * q_scale
            k = k_ref[bi].astype(F32)
            t["q_dec"] = (q * eb).astype(BF16)
            t["k_grow"] = (k * enb).astype(BF16)
            t["q_grow"] = (q * enb).astype(BF16)
            t["k_dec"] = (k * eb).astype(BF16)
            t["k_in"] = (k * jnp.exp(b_last_rows - b)).astype(BF16)

        def intra():
            scores = jnp.where(lower, _dot_nt(t["q_dec"], t["k_grow"]),
                               jnp.where(upper, _dot_nt(t["q_grow"], t["k_dec"]), 0.0))
            t["o_intra"] = _dot(scores.astype(BF16), v_ref[bi])

        def state():
            v = v_ref[bi]
            incs = [_dot_tn(v[rows, :], t["k_in"][rows, :]) for rows in chunk_rows]
            states = [st_ref[bi]]
            for ci in range(n_chunks):
                states.append(jnp.exp(t["b_last"][ci]) * states[ci] + incs[ci])
            st_ref[bi] = states[n_chunks]
            t["states"] = states

        def readout():
            o = jnp.concatenate(
                [t["o_intra"][rows, :] + _dot_nt(t["q_dec"][rows, :], t["states"][ci].astype(BF16))
                 for ci, rows in enumerate(chunk_rows)], axis=0)
            o = o * lax.rsqrt(jnp.mean(o * o, axis=-1, keepdims=True) + RMS_EPS) * nw_ref[...]
            g = g_ref[bi].astype(F32)
            o_ref[bi] = (o * (g * _sigmoid(g))).astype(o_ref.dtype)

        return [gate, cumsum, decays, intra, state, readout]

    for f in _interleave(*[chain(bi) for bi in range(nb)]):
        f()


def _gla_core(proj, z, wgk, bgk, norm_w, *, batch, seq_len, tc, name):
    M, n_proj = proj.shape
    kd = wgk.shape[1]
    vd = norm_w.shape[0]
    H = GLA_HEADS
    dk, dv = kd // H, vd // H
    nt = seq_len // tc
    proj3 = proj.reshape(batch, seq_len, n_proj)
    y = pl.pallas_call(
        functools.partial(_gla_kernel, n_chunks=tc // CHUNK, q_scale=dk ** -0.5),
        grid=(H, nt),
        in_specs=[
            pl.BlockSpec((batch, tc, dk), lambda h, n: (0, n, h)),
            pl.BlockSpec((batch, tc, dk), lambda h, n: (0, n, kd // dk + h)),
            pl.BlockSpec((batch, tc, dv), lambda h, n: (0, n, 2 * kd // dv + h)),
            pl.BlockSpec((batch, tc, dv), lambda h, n: (0, n, (2 * kd + vd) // dv + h)),
            pl.BlockSpec((batch, tc, z.shape[1]), lambda h, n: (0, n, 0)),
            pl.BlockSpec((wgk.shape[0], dk), lambda h, n: (0, h)),
            pl.BlockSpec((1, dk), lambda h, n: (0, h)),
            pl.BlockSpec((1, dv), lambda h, n: (0, h)),
        ],
        out_specs=pl.BlockSpec((batch, tc, dv), lambda h, n: (0, n, h)),
        out_shape=jax.ShapeDtypeStruct((batch, seq_len, vd), BF16),
        scratch_shapes=[pltpu.VMEM((batch, dv, dk), F32)],
        compiler_params=_params(("arbitrary", "arbitrary")),
        name=name,
    )(proj3, proj3, proj3, proj3, z.reshape(batch, seq_len, z.shape[1]), wgk,
      bgk.reshape(1, kd), norm_w.reshape(1, vd))
    return y.reshape(M, vd)


def _gla_layer_kernel(x_ref, nw_ref, w_ref, wz_ref, wgk_ref, bgk_ref, gnw_ref, o_ref,
                      h_ref, st_ref, z_ref, q_ref, k_ref, v_ref, g_ref, *, heads, q_scale):
    nb, tc, dm = x_ref.shape
    kd = wgk_ref.shape[1]
    vd = o_ref.shape[2]
    dk, dv = kd // heads, vd // heads
    n_chunks = tc // CHUNK

    @pl.when(pl.program_id(0) == 0)
    def _():
        st_ref[...] = jnp.zeros_like(st_ref)

    for bi in range(nb):
        x = x_ref[bi]
        ms = jnp.mean(x * x, axis=-1, keepdims=True)
        h_ref[bi * tc:(bi + 1) * tc, :] = (x * lax.rsqrt(ms + RMS_EPS) * nw_ref[...]).astype(BF16)
    z_ref[...] = _dot(h_ref[...], wz_ref[...]).astype(BF16)

    r, c, same = _tile_masks(tc)
    lower = same & (r >= c)
    upper = same & (r < c)
    tri = lower.astype(BF16)
    chunk_rows = [slice(ci * CHUNK, (ci + 1) * CHUNK) for ci in range(n_chunks)]

    def proj_steps(hd):
        p = hd % 2
        steps = []
        for dst, col0, width in ((q_ref, hd * dk, dk), (k_ref, kd + hd * dk, dk),
                                 (v_ref, 2 * kd + hd * dv, dv), (g_ref, 2 * kd + vd + hd * dv, dv)):
            for c0 in range(0, width, MXU_TILE):
                acc = {}

                def kdot(kb, col=col0 + c0, acc=acc):
                    def f():
                        ks = pl.ds(kb * MXU_TILE, MXU_TILE)
                        d = _dot(h_ref[:, ks], w_ref[ks, pl.ds(col, MXU_TILE)])
                        acc["v"] = d if kb == 0 else acc["v"] + d
                    return f

                def park(dst=dst, c0=c0, acc=acc):
                    def f():
                        dst[p, :, c0:c0 + MXU_TILE] = acc["v"].astype(dst.dtype)
                    return f

                steps += [kdot(kb) for kb in range(dm // MXU_TILE)] + [park()]
        return steps

    def chain(hd, bi):
        p = hd % 2
        rows_b = slice(bi * tc, (bi + 1) * tc)
        kcols = slice(hd * dk, (hd + 1) * dk)
        vcols = slice(hd * dv, (hd + 1) * dv)
        t = {}

        def gate():
            pre = _dot(z_ref[rows_b, :], wgk_ref[:, kcols]) + bgk_ref[:, kcols]
            log_a = -_softplus(-pre) / GLA_GATE_TAU
            t["hi"] = log_a.astype(BF16)
            t["lo"] = (log_a - t["hi"].astype(F32)).astype(BF16)

        def cumsum():
            b = _dot(tri, t["hi"]) + _dot(tri, t["lo"])
            t["b"] = b
            t["b_last"] = [b[(ci + 1) * CHUNK - 1:(ci + 1) * CHUNK, :] for ci in range(n_chunks)]

        def decays():
            b = t["b"]
            b_last_rows = jnp.concatenate(
                [jnp.broadcast_to(bl, (CHUNK, dk)) for bl in t["b_last"]], axis=0)
            eb = jnp.exp(b)
            enb = jnp.exp(-b)
            q = q_ref[p, rows_b, :] * q_scale
            k = k_ref[p, rows_b, :]
            t["q_dec"] = (q * eb).astype(BF16)
            t["k_grow"] = (k * enb).astype(BF16)
            t["q_grow"] = (q * enb).astype(BF16)
            t["k_dec"] = (k * eb).astype(BF16)
            t["k_in"] = (k * jnp.exp(b_last_rows - b)).astype(BF16)

        def intra():
            scores = jnp.where(lower, _dot_nt(t["q_dec"], t["k_grow"]),
                               jnp.where(upper, _dot_nt(t["q_grow"], t["k_dec"]), 0.0))
            t["o_intra"] = _dot(scores.astype(BF16), v_ref[p, rows_b, :])

        def state():
            v = v_ref[p, rows_b, :]
            incs = [_dot_tn(v[rows, :], t["k_in"][rows, :]) for rows in chunk_rows]
            states = [st_ref[bi, hd]]
            for ci in range(n_chunks):
                states.append(jnp.exp(t["b_last"][ci]) * states[ci] + incs[ci])
            st_ref[bi, hd] = states[n_chunks]
            t["states"] = states

        def readout():
            o = jnp.concatenate(
                [t["o_intra"][rows, :] + _dot_nt(t["q_dec"][rows, :], t["states"][ci].astype(BF16))
                 for ci, rows in enumerate(chunk_rows)], axis=0)
            o = o * lax.rsqrt(jnp.mean(o * o, axis=-1, keepdims=True) + RMS_EPS) * gnw_ref[:, vcols]
            g = g_ref[p, rows_b, :]
            o_ref[bi, :, vcols] = (o * (g * _sigmoid(g))).astype(o_ref.dtype)

        return [gate, cumsum, decays, intra, state, readout]

    for f in proj_steps(0):
        f()
    for hd in range(heads):
        slot = [chain(hd, bi) for bi in range(nb)]
        if hd + 1 < heads:
            slot.append(proj_steps(hd + 1))
        for f in _interleave(*slot):
            f()


def _gla_layer(x3, nw, w_in, layer, wgk, bgk, gnorm_w, *, tc, name):
    B, S, D = x3.shape
    kd = wgk.shape[1]
    vd = gnorm_w.shape[0]
    H = GLA_HEADS
    dk, dv = kd // H, vd // H
    rows = B * tc
    const = lambda n: (0, 0)
    return pl.pallas_call(
        functools.partial(_gla_layer_kernel, heads=H, q_scale=dk ** -0.5),
        grid=(S // tc,),
        in_specs=[
            pl.BlockSpec((B, tc, D), lambda n: (0, n, 0)),
            pl.BlockSpec((1, D), const),
            pl.BlockSpec((None, D, 2 * kd + 2 * vd), lambda n: (layer, 0, 0),
                         pipeline_mode=pl.Buffered(1)),
            pl.BlockSpec((None, D, LANES), lambda n: (layer, 0, (2 * kd + 2 * vd) // LANES)),
            pl.BlockSpec(wgk.shape, const),
            pl.BlockSpec((1, kd), const),
            pl.BlockSpec((1, vd), const),
        ],
        out_specs=pl.BlockSpec((B, tc, vd), lambda n: (0, n, 0)),
        out_shape=jax.ShapeDtypeStruct((B, S, vd), BF16),
        scratch_shapes=[
            pltpu.VMEM((rows, D), BF16),
            pltpu.VMEM((B, H, dv, dk), F32),
            pltpu.VMEM((rows, LANES), BF16),
            pltpu.VMEM((2, rows, dk), F32),
            pltpu.VMEM((2, rows, dk), F32),
            pltpu.VMEM((2, rows, dv), BF16),
            pltpu.VMEM((2, rows, dv), F32),
        ],
        compiler_params=_params(("arbitrary",)),
        name=name,
    )(x3, nw.reshape(1, D), w_in, w_in, wgk, bgk.reshape(1, kd), gnorm_w.reshape(1, vd))


def _rope_kernel(cos_ref, sin_ref):
    tt, half = cos_ref.shape
    pos = (pl.program_id(0) * tt + lax.broadcasted_iota(jnp.int32, (tt, half), 0)).astype(F32)
    jj = lax.broadcasted_iota(jnp.int32, (tt, half), 1).astype(F32)
    inv = jnp.exp(-(jj / half) * math.log(ROPE_BASE))
    ang = pos * inv
    cos_ref[...] = jnp.cos(ang)
    sin_ref[...] = jnp.sin(ang)


def _rope_tables(seq_len, half, *, tt):
    return pl.pallas_call(
        _rope_kernel,
        grid=(seq_len // tt,),
        out_specs=[pl.BlockSpec((tt, half), lambda i: (i, 0))] * 2,
        out_shape=[jax.ShapeDtypeStruct((seq_len, half), F32)] * 2,
        compiler_params=_params(("parallel",)),
        name="rope_tables",
    )()


def _ret_layer_kernel(x_ref, nw_ref, wq_ref, wk_ref, wv_ref, wg_ref, cos_ref, sin_ref, rnw_ref, o_ref,
                      h_ref, st_ref, dmat_ref, xi_ref, zeta_ref, q_ref, k_ref, v_ref, g_ref, *,
                      heads_per_step, k_scale):
    nb, tc, dm = x_ref.shape
    dk = wq_ref.shape[1] // heads_per_step
    dv = wv_ref.shape[1] // heads_per_step
    half = dk // 2
    n = pl.program_id(0)
    hp = pl.program_id(1)

    def log_gamma(hd, shape):
        return jnp.log(1.0 - jnp.exp2(-5.0 - jnp.full(shape, hd.astype(F32), F32)))

    @pl.when(hp == 0)
    def _():
        for bi in range(nb):
            x = x_ref[bi]
            ms = jnp.mean(x * x, axis=-1, keepdims=True)
            h_ref[bi * tc:(bi + 1) * tc, :] = (x * lax.rsqrt(ms + RMS_EPS) * nw_ref[...]).astype(BF16)

    @pl.when(n == 0)
    def _():
        r, c, _ = _tile_masks(tc)
        shift = CHUNK.bit_length() - 1
        reads = lax.shift_right_logical(c, shift) <= lax.shift_right_logical(r, shift)
        dist = jnp.abs(r - c).astype(F32)
        pos = lax.broadcasted_iota(jnp.int32, (tc, dk), 0).astype(F32)
        for hh in range(heads_per_step):
            hd = hp * heads_per_step + hh
            for bi in range(nb):
                st_ref[bi, hd] = jnp.zeros((dv, dk), F32)
            dmat_ref[hd] = jnp.where(reads, jnp.exp(log_gamma(hd, (tc, tc)) * dist), 0.0)
            lg = log_gamma(hd, (tc, dk))
            xi_ref[hd] = jnp.exp(lg * (pos + 1.0))
            zeta_ref[hd] = jnp.exp(lg * (tc - 1.0 - pos))

    def proj_steps(hh):
        steps = []
        for w_ref, dst, width in ((wq_ref, q_ref, dk), (wk_ref, k_ref, dk), (wv_ref, v_ref, dv),
                                  (wg_ref, g_ref, dv)):
            for c0 in range(0, width, MXU_TILE):
                acc = {}

                def kdot(kb, w_ref=w_ref, col=hh * width + c0, acc=acc):
                    def f():
                        ks = pl.ds(kb * MXU_TILE, MXU_TILE)
                        d = _dot(h_ref[:, ks], w_ref[ks, col:col + MXU_TILE])
                        acc["v"] = d if kb == 0 else acc["v"] + d
                    return f

                def park(dst=dst, c0=c0, acc=acc):
                    def f():
                        dst[hh, :, c0:c0 + MXU_TILE] = acc["v"].astype(dst.dtype)
                    return f

                steps += [kdot(kb) for kb in range(dm // MXU_TILE)] + [park()]
        return steps

    def rope(x):
        x1, x2 = x[:, :half], x[:, half:]
        cos, sin = cos_ref[...], sin_ref[...]
        return jnp.concatenate([x1 * cos - x2 * sin, x2 * cos + x1 * sin], axis=-1)

    def chain(hh, bi):
        hd = hp * heads_per_step + hh
        rows = slice(bi * tc, (bi + 1) * tc)
        t = {}

        def rotate():
            t["q"] = rope(q_ref[hh, rows, :])
            t["k"] = rope(k_ref[hh, rows, :]) * k_scale

        def scores():
            t["s"] = (_dot_nt(t["q"].astype(BF16), t["k"].astype(BF16)) * dmat_ref[hd]).astype(BF16)

        def mix():
            st = st_ref[bi, hd]
            v = v_ref[hh, rows, :]
            t["o"] = _dot(t["s"], v) + _dot_nt((t["q"] * xi_ref[hd]).astype(BF16), st.astype(BF16))
            decay = jnp.exp(log_gamma(hd, (1, dk)) * float(tc))
            st_ref[bi, hd] = decay * st + _dot_tn(v, (t["k"] * zeta_ref[hd]).astype(BF16))

        def finish():
            o = t["o"]
            oc = o - jnp.mean(o, axis=-1, keepdims=True)
            o = (oc * lax.rsqrt(jnp.mean(oc * oc, axis=-1, keepdims=True) + RMS_EPS)
                 * rnw_ref[:, hh * dv:(hh + 1) * dv])
            g = g_ref[hh, rows, :]
            o_ref[bi, :, hh * dv:(hh + 1) * dv] = (o * (g * _sigmoid(g))).astype(o_ref.dtype)

        return [rotate, scores, mix, finish]

    for f in proj_steps(0):
        f()
    for hh in range(heads_per_step):
        slot = [chain(hh, bi) for bi in range(nb)]
        if hh + 1 < heads_per_step:
            slot.append(proj_steps(hh + 1))
        for f in _interleave(*slot):
            f()


def _ret_layer(x3, nw, w_in, layer, cos, sin, norm_w, *, tc, heads_per_step, name):
    B, S, D = x3.shape
    vd = norm_w.shape[0]
    H = RET_HEADS
    dv = vd // H
    dk = cos.shape[1] * 2
    kd = H * dk
    rows = B * tc
    hs = heads_per_step
    np_ = H // hs
    return pl.pallas_call(
        functools.partial(_ret_layer_kernel, heads_per_step=hs, k_scale=dk ** -0.5),
        grid=(S // tc, np_),
        in_specs=[
            pl.BlockSpec((B, tc, D), lambda n, p: (0, n, 0)),
            pl.BlockSpec((1, D), lambda n, p: (0, 0)),
            pl.BlockSpec((None, D, hs * dk), lambda n, p: (layer, 0, p)),
            pl.BlockSpec((None, D, hs * dk), lambda n, p: (layer, 0, np_ + p)),
            pl.BlockSpec((None, D, hs * dv), lambda n, p: (layer, 0, 2 * kd // (hs * dv) + p)),
            pl.BlockSpec((None, D, hs * dv), lambda n, p: (layer, 0, (2 * kd + vd) // (hs * dv) + p)),
            pl.BlockSpec((tc, dk // 2), lambda n, p: (n, 0)),
            pl.BlockSpec((tc, dk // 2), lambda n, p: (n, 0)),
            pl.BlockSpec((1, hs * dv), lambda n, p: (0, p)),
        ],
        out_specs=pl.BlockSpec((B, tc, hs * dv), lambda n, p: (0, n, p)),
        out_shape=jax.ShapeDtypeStruct((B, S, vd), BF16),
        scratch_shapes=[
            pltpu.VMEM((rows, D), BF16),
            pltpu.VMEM((B, H, dv, dk), F32),
            pltpu.VMEM((H, tc, tc), F32),
            pltpu.VMEM((H, tc, dk), F32),
            pltpu.VMEM((H, tc, dk), F32),
            pltpu.VMEM((hs, rows, dk), F32),
            pltpu.VMEM((hs, rows, dk), F32),
            pltpu.VMEM((hs, rows, dv), BF16),
            pltpu.VMEM((hs, rows, dv), F32),
        ],
        compiler_params=_params(("arbitrary", "arbitrary")),
        name=name,
    )(x3, nw.reshape(1, D), w_in, w_in, w_in, w_in, cos, sin, norm_w.reshape(1, vd))


def _ret_kernel(q_ref, k_ref, v_ref, g_ref, cos_ref, sin_ref, nw_ref, o_ref,
                st_ref, dmat_ref, xi_ref, zeta_ref, *, k_scale):
    nb, tc, dk = q_ref.shape
    half = dk // 2
    head = pl.program_id(0).astype(F32)

    def log_gamma(shape):
        return jnp.log(1.0 - jnp.exp2(-5.0 - jnp.full(shape, head, F32)))

    @pl.when(pl.program_id(1) == 0)
    def _():
        st_ref[...] = jnp.zeros_like(st_ref)
        r, c, _ = _tile_masks(tc)
        shift = CHUNK.bit_length() - 1
        reads = lax.shift_right_logical(c, shift) <= lax.shift_right_logical(r, shift)
        dist = jnp.abs(r - c).astype(F32)
        dmat_ref[...] = jnp.where(reads, jnp.exp(log_gamma((tc, tc)) * dist), 0.0)
        pos = lax.broadcasted_iota(jnp.int32, (tc, dk), 0).astype(F32)
        lg = log_gamma((tc, dk))
        xi_ref[...] = jnp.exp(lg * (pos + 1.0))
        zeta_ref[...] = jnp.exp(lg * (tc - 1.0 - pos))

    decay = jnp.exp(log_gamma((1, dk)) * float(tc))

    def rope(x):
        x1, x2 = x[:, :half], x[:, half:]
        cos, sin = cos_ref[...], sin_ref[...]
        return jnp.concatenate([x1 * cos - x2 * sin, x2 * cos + x1 * sin], axis=-1)

    def chain(bi):
        t = {}

        def rotate():
            t["q"] = rope(q_ref[bi].astype(F32))
            t["k"] = rope(k_ref[bi].astype(F32)) * k_scale

        def scores():
            t["s"] = (_dot_nt(t["q"].astype(BF16), t["k"].astype(BF16)) * dmat_ref[...]).astype(BF16)

        def mix():
            st = st_ref[bi]
            v = v_ref[bi]
            t["o"] = _dot(t["s"], v) + _dot_nt((t["q"] * xi_ref[...]).astype(BF16), st.astype(BF16))
            st_ref[bi] = decay * st + _dot_tn(v, (t["k"] * zeta_ref[...]).astype(BF16))

        def finish():
            o = t["o"]
            oc = o - jnp.mean(o, axis=-1, keepdims=True)
            o = oc * lax.rsqrt(jnp.mean(oc * oc, axis=-1, keepdims=True) + RMS_EPS) * nw_ref[...]
            g = g_ref[bi].astype(F32)
            o_ref[bi] = (o * (g * _sigmoid(g))).astype(o_ref.dtype)

        return [rotate, scores, mix, finish]

    for f in _interleave(*[chain(bi) for bi in range(nb)]):
        f()


def _ret_core(proj, cos, sin, norm_w, *, batch, seq_len, tc, name):
    M, n_proj = proj.shape
    vd = norm_w.shape[0]
    H = RET_HEADS
    dv = vd // H
    dk = cos.shape[1] * 2
    kd = H * dk
    proj3 = proj.reshape(batch, seq_len, n_proj)
    y = pl.pallas_call(
        functools.partial(_ret_kernel, k_scale=dk ** -0.5),
        grid=(H, seq_len // tc),
        in_specs=[
            pl.BlockSpec((batch, tc, dk), lambda h, n: (0, n, h)),
            pl.BlockSpec((batch, tc, dk), lambda h, n: (0, n, H + h)),
            pl.BlockSpec((batch, tc, dv), lambda h, n: (0, n, 2 * kd // dv + h)),
            pl.BlockSpec((batch, tc, dv), lambda h, n: (0, n, (2 * kd + vd) // dv + h)),
            pl.BlockSpec((tc, dk // 2), lambda h, n: (n, 0)),
            pl.BlockSpec((tc, dk // 2), lambda h, n: (n, 0)),
            pl.BlockSpec((1, dv), lambda h, n: (0, h)),
        ],
        out_specs=pl.BlockSpec((batch, tc, dv), lambda h, n: (0, n, h)),
        out_shape=jax.ShapeDtypeStruct((batch, seq_len, vd), BF16),
        scratch_shapes=[
            pltpu.VMEM((batch, dv, dk), F32),
            pltpu.VMEM((tc, tc), F32),
            pltpu.VMEM((tc, dk), F32),
            pltpu.VMEM((tc, dk), F32),
        ],
        compiler_params=_params(("arbitrary", "arbitrary")),
        name=name,
    )(proj3, proj3, proj3, proj3, cos, sin, norm_w.reshape(1, vd))
    return y.reshape(M, vd)


def _lru_kernel(xb_ref, yb_ref, cw_ref, cb_ref, wga_ref, bga_ref, wgx_ref, bgx_ref, lam_ref, o_ref,
                hcar_ref, ccar_ref, a_ref, u_ref, hs_ref, *, n_blocks):
    tt, width = xb_ref.shape
    kw = cw_ref.shape[0]

    @pl.when(pl.program_id(1) == 0)
    def _():
        hcar_ref[...] = jnp.zeros_like(hcar_ref)
        ccar_ref[...] = jnp.zeros_like(ccar_ref)

    bw = width // n_blocks
    for nb in range(n_blocks):
        cs = pl.ds(nb * bw, bw)
        xb = xb_ref[:, cs].astype(F32)
        ext = jnp.concatenate([ccar_ref[:, cs], xb], axis=0)
        ccar_ref[:, cs] = xb[tt - SUBLANES:, :]
        cw = cw_ref[:, cs]
        xc = cw[kw - 1:kw, :] * xb + cb_ref[:, cs]
        for d in range(1, kw):
            xc = xc + cw[kw - 1 - d:kw - d, :] * ext[SUBLANES - d:SUBLANES - d + tt, :]
        xcb = xc.astype(BF16)
        rg = _sigmoid(_dot(xcb, wga_ref[nb]) + bga_ref[:, cs])
        ig = _sigmoid(_dot(xcb, wgx_ref[nb]) + bgx_ref[:, cs])
        a = jnp.exp(rg * (-LRU_C * _softplus(-lam_ref[:, cs])))
        a_ref[:, cs] = a
        u_ref[:, cs] = xc * ig * jnp.sqrt(1.0 - a * a)

    def step(t, h):
        h = a_ref[pl.ds(t, 1), :] * h + u_ref[pl.ds(t, 1), :]
        hs_ref[pl.ds(t, 1), :] = h
        return h

    hcar_ref[0:1, :] = lax.fori_loop(0, tt, step, hcar_ref[0:1, :], unroll=8)

    for nb in range(n_blocks):
        cs = pl.ds(nb * bw, bw)
        o_ref[:, cs] = (hs_ref[:, cs] * _gelu_tanh(yb_ref[:, cs].astype(F32))).astype(o_ref.dtype)


def _lru_core(proj, conv_w, conv_b, w_ga, b_ga, w_gx, b_gx, lam, layer, *, batch, seq_len, tt, name):
    M = proj.shape[0]
    width = conv_w.shape[1]
    nt = seq_len // tt
    nb = w_ga.shape[1]
    vec = lambda: pl.BlockSpec((1, width), lambda b, n: (0, 0))
    wblk = lambda: pl.BlockSpec((None,) + w_ga.shape[1:], lambda b, n: (layer, 0, 0, 0))
    return pl.pallas_call(
        functools.partial(_lru_kernel, n_blocks=nb),
        grid=(batch, nt),
        in_specs=[
            pl.BlockSpec((tt, width), lambda b, n: (b * nt + n, 0)),
            pl.BlockSpec((tt, width), lambda b, n: (b * nt + n, 1)),
            pl.BlockSpec(conv_w.shape, lambda b, n: (0, 0)),
            vec(), wblk(), vec(), wblk(), vec(), vec(),
        ],
        out_specs=pl.BlockSpec((tt, width), lambda b, n: (b * nt + n, 0)),
        out_shape=jax.ShapeDtypeStruct((M, width), BF16),
        scratch_shapes=[
            pltpu.VMEM((SUBLANES, width), F32),
            pltpu.VMEM((SUBLANES, width), F32),
            pltpu.VMEM((tt, width), F32),
            pltpu.VMEM((tt, width), F32),
            pltpu.VMEM((tt, width), F32),
        ],
        compiler_params=_params(("arbitrary", "arbitrary")),
        name=name,
    )(proj, proj, conv_w, conv_b.reshape(1, width), w_ga, b_ga.reshape(1, width),
      w_gx, b_gx.reshape(1, width), lam.reshape(1, width))


def _lru_layer_kernel(x_ref, nw_ref, w_ref, cw_ref, cb_ref, wga_ref, bga_ref, wgx_ref, bgx_ref,
                      lam_ref, o_ref, h_ref, hcar_ref, ccar_ref, xb_ref, yb_ref, a_ref, u_ref, hs_ref,
                      gy_ref, *, n_blocks):
    tt, dm = x_ref.shape
    width = o_ref.shape[1]
    kw = cw_ref.shape[0]
    bw = width // n_blocks

    @pl.when(pl.program_id(1) == 0)
    def _():
        hcar_ref[...] = jnp.zeros_like(hcar_ref)
        ccar_ref[...] = jnp.zeros_like(ccar_ref)

    x = x_ref[...]
    ms = jnp.mean(x * x, axis=-1, keepdims=True)
    h_ref[...] = (x * lax.rsqrt(ms + RMS_EPS) * nw_ref[...]).astype(BF16)

    def proj_steps(nb):
        p = nb % 2
        steps = []
        for dst, col in ((xb_ref, nb * bw), (yb_ref, width + nb * bw)):
            acc = {}

            def kdot(kb, col=col, acc=acc):
                def f():
                    ks = pl.ds(kb * MXU_TILE, MXU_TILE)
                    d = _dot(h_ref[:, ks], w_ref[ks, pl.ds(col, bw)])
                    acc["v"] = d if kb == 0 else acc["v"] + d
                return f

            def park(dst=dst, acc=acc):
                def f():
                    dst[p] = acc["v"]
                return f

            steps += [kdot(kb) for kb in range(dm // MXU_TILE)] + [park()]
        return steps

    def gate_steps(nb):
        p = nb % 2
        cs = pl.ds(nb * bw, bw)
        t = {}

        def conv():
            xb = xb_ref[p]
            ext = jnp.concatenate([ccar_ref[:, cs], xb], axis=0)
            ccar_ref[:, cs] = xb[tt - SUBLANES:, :]
            cw = cw_ref[:, cs]
            xc = cw[kw - 1:kw, :] * xb + cb_ref[:, cs]
            for d in range(1, kw):
                xc = xc + cw[kw - 1 - d:kw - d, :] * ext[SUBLANES - d:SUBLANES - d + tt, :]
            t["xc"] = xc

        def gates():
            xc = t["xc"]
            xcb = xc.astype(BF16)
            rg = _sigmoid(_dot(xcb, wga_ref[nb]) + bga_ref[:, cs])
            ig = _sigmoid(_dot(xcb, wgx_ref[nb]) + bgx_ref[:, cs])
            a = jnp.exp(rg * (-LRU_C * _softplus(-lam_ref[:, cs])))
            a_ref[:, cs] = a
            y = 1.0 - a * a
            u_ref[:, cs] = xc * ig * jnp.where(y > 0.0, y * lax.rsqrt(y), 0.0)

        def out_gate():
            gy_ref[:, cs] = _gelu_tanh(yb_ref[p])

        return [conv, gates, out_gate]

    for f in proj_steps(0):
        f()
    for nb in range(n_blocks):
        slot = [gate_steps(nb)] + ([proj_steps(nb + 1)] if nb + 1 < n_blocks else [])
        for f in _interleave(*slot):
            f()

    def step(t, h):
        h = a_ref[pl.ds(t, 1), :] * h + u_ref[pl.ds(t, 1), :]
        hs_ref[pl.ds(t, 1), :] = h
        return h

    hcar_ref[0:1, :] = lax.fori_loop(0, tt, step, hcar_ref[0:1, :], unroll=8)

    for nb in range(n_blocks):
        cs = pl.ds(nb * bw, bw)
        o_ref[:, cs] = (hs_ref[:, cs] * gy_ref[:, cs]).astype(o_ref.dtype)


def _lru_layer(x, nw, w_in, conv_w, conv_b, w_ga, b_ga, w_gx, b_gx, lam, layer, *, batch, seq_len, tt,
               name):
    M, D = x.shape
    width = conv_w.shape[1]
    nt = seq_len // tt
    nb = w_ga.shape[1]
    bw = width // nb
    vec = lambda: pl.BlockSpec((1, width), lambda b, n: (0, 0))
    wblk = lambda: pl.BlockSpec((None,) + w_ga.shape[1:], lambda b, n: (layer, 0, 0, 0))
    return pl.pallas_call(
        functools.partial(_lru_layer_kernel, n_blocks=nb),
        grid=(batch, nt),
        in_specs=[
            pl.BlockSpec((tt, D), lambda b, n: (b * nt + n, 0)),
            pl.BlockSpec((1, D), lambda b, n: (0, 0)),
            pl.BlockSpec((None, D, 2 * width), lambda b, n: (layer, 0, 0),
                         pipeline_mode=pl.Buffered(1)),
            pl.BlockSpec(conv_w.shape, lambda b, n: (0, 0)),
            vec(), wblk(), vec(), wblk(), vec(), vec(),
        ],
        out_specs=pl.BlockSpec((tt, width), lambda b, n: (b * nt + n, 0)),
        out_shape=jax.ShapeDtypeStruct((M, width), BF16),
        scratch_shapes=[
            pltpu.VMEM((tt, D), BF16),
            pltpu.VMEM((SUBLANES, width), F32),
            pltpu.VMEM((SUBLANES, width), F32),
            pltpu.VMEM((2, tt, bw), F32),
            pltpu.VMEM((2, tt, bw), F32),
            pltpu.VMEM((tt, width), F32),
            pltpu.VMEM((tt, width), F32),
            pltpu.VMEM((tt, width), F32),
            pltpu.VMEM((tt, width), F32),
        ],
        compiler_params=_params(("arbitrary", "arbitrary")),
        name=name,
    )(x, nw.reshape(1, D), w_in, conv_w, conv_b.reshape(1, width), w_ga, b_ga.reshape(1, width),
      w_gx, b_gx.reshape(1, width), lam.reshape(1, width))


def kernel(x, norm_mix_w, norm_ffn_w, norm_out_w, gla_w_in, gla_w_gk, gla_b_gk, gla_norm_w, gla_w_out, lru_w_in, lru_conv_w, lru_conv_b, lru_w_ga, lru_b_ga, lru_w_gx, lru_b_gx, lru_lambda, lru_w_out, ret_w_in, ret_norm_w, ret_w_out, ffn_w_up, ffn_conv_w, ffn_conv_b, ffn_w_down):
    B, S, D = x.shape
    M = B * S
    depth = norm_mix_w.shape[0]
    xf = x.reshape(M, D)

    gla_kd = gla_w_gk.shape[2]
    gla_vd = gla_norm_w.shape[1]
    gla_main = 2 * gla_kd + 2 * gla_vd
    rank = gla_w_gk.shape[1]
    ret_dk = ret_w_in.shape[2] // (6 * RET_HEADS)
    cos = sin = None
    tm_p = min(1024, M)
    tm_f = min(512, S)
    tc = min(256, S)
    tt_lru = min(512, S)
    tc_ret = min(512, S)

    gla_w_in_b = jnp.pad(gla_w_in, ((0, 0), (0, 0), (0, LANES - rank))).astype(BF16)
    gla_w_out_b = gla_w_out.astype(BF16)
    lru_w_in_b, lru_w_out_b = lru_w_in.astype(BF16), lru_w_out.astype(BF16)
    lru_w_ga_b, lru_w_gx_b = lru_w_ga.astype(BF16), lru_w_gx.astype(BF16)
    ret_w_in_b, ret_w_out_b = ret_w_in.astype(BF16), ret_w_out.astype(BF16)
    ffn_w_up_b, ffn_w_down_b = ffn_w_up[:1].astype(BF16), ffn_w_down[:1].astype(BF16)

    for i in range(depth):
        kind, l = i % N_MIXERS, i // N_MIXERS
        if kind == 0:
            w_gk = jnp.pad(gla_w_gk[l].astype(BF16), ((0, LANES - rank), (0, 0)))
            y = _gla_layer(xf.reshape(B, S, D), norm_mix_w[i], gla_w_in_b, l, w_gk, gla_b_gk[l],
                           gla_norm_w[l], tc=tc, name=f"gla_mix_{i}")
            xf = _proj_res(y.reshape(M, gla_vd), gla_w_out_b, l, xf, tm=tm_p, tn=1024,
                           name=f"gla_out_{i}")
        elif kind == 1:
            y = _lru_layer(xf, norm_mix_w[i], lru_w_in_b, lru_conv_w[l], lru_conv_b[l], lru_w_ga_b,
                           lru_b_ga[l], lru_w_gx_b, lru_b_gx[l], lru_lambda[l], l,
                           batch=B, seq_len=S, tt=tt_lru, name=f"lru_mix_{i}")
            xf = _proj_res(y, lru_w_out_b, l, xf, tm=tm_p, tn=1024, name=f"lru_out_{i}")
        else:
            if cos is None:
                cos, sin = _rope_tables(S, ret_dk // 2, tt=tm_f)
            y = _ret_layer(xf.reshape(B, S, D), norm_mix_w[i], ret_w_in_b, l, cos, sin, ret_norm_w[l],
                           tc=tc, heads_per_step=2, name=f"ret_mix_{i}")
            xf = _proj_res(y.reshape(M, ret_w_out.shape[1]), ret_w_out_b, l, xf, tm=tm_p, tn=512,
                           name=f"ret_out_{i}")
        last = i == depth - 1
        res = _ffn(xf, norm_ffn_w[i], ffn_w_up_b, ffn_conv_w[i], ffn_conv_b[i], ffn_w_down_b, 0,
                   norm_out_w if last else None, None if last else (ffn_w_up, ffn_w_down, i + 1),
                   seq_len=S, tm=tm_f, tf=1024, sub=256, name=f"ffn_{i}")
        if last:
            xf = res
        else:
            xf, ffn_w_up_b, ffn_w_down_b = res
    return xf.reshape(B, S, D)
```

```python
import functools
import math

import jax
import jax.numpy as jnp
from jax import lax
from jax.experimental import pallas as pl
from jax.experimental.pallas import tpu as pltpu

F32 = jnp.float32
BF16 = jnp.bfloat16

CHUNK = 64
RMS_EPS = 1e-6
N_MIXERS = 3
GLA_HEADS = 4
GLA_GATE_TAU = 16.0
LRU_BLOCK_W = 256
LRU_C = 8.0
RET_HEADS = 8
ROPE_BASE = 10000.0

LANES = 128
SUBLANES = 8
MXU_TILE = 256
ROW_BLK = 128
VMEM_LIMIT = 56 * 1024 * 1024


def _params(sem):
    return pltpu.CompilerParams(dimension_semantics=sem, vmem_limit_bytes=VMEM_LIMIT)


def _sigmoid(x):
    return 0.5 * jnp.tanh(0.5 * x) + 0.5


def _gelu_tanh(x):
    c = math.sqrt(2.0 / math.pi)
    return x * (0.5 * (1.0 + jnp.tanh(c * (x + 0.044715 * (x * x * x)))))


def _softplus(x):
    return jnp.maximum(x, 0.0) + jnp.log1p(jnp.exp(-jnp.abs(x)))


def _dot(a, b):
    return jnp.dot(a, b, preferred_element_type=F32)


def _dot_nt(a, b):
    return lax.dot_general(a, b, (((1,), (1,)), ((), ())), preferred_element_type=F32)


def _dot_tn(a, b):
    return lax.dot_general(a, b, (((0,), (0,)), ((), ())), preferred_element_type=F32)


def _norm_proj_kernel(x_ref, nw_ref, w_ref, *rest, has_extra):
    if has_extra:
        we_ref, o_ref, oe_ref, h_ref = rest
    else:
        o_ref, h_ref = rest

    @pl.when(pl.program_id(1) == 0)
    def _():
        x = x_ref[...]
        ms = jnp.mean(x * x, axis=-1, keepdims=True)
        h_ref[...] = (x * lax.rsqrt(ms + RMS_EPS) * nw_ref[...]).astype(BF16)
        if has_extra:
            oe_ref[...] = _dot(h_ref[...], we_ref[...]).astype(oe_ref.dtype)

    o_ref[...] = _dot(h_ref[...], w_ref[...]).astype(o_ref.dtype)


def _norm_proj(x, nw, w, layer, w_extra=None, *, n_out=None, tm, tn, name):
    M, D = x.shape
    N = w.shape[2] if n_out is None else n_out
    has_extra = w_extra is not None
    in_specs = [
        pl.BlockSpec((tm, D), lambda i, j: (i, 0)),
        pl.BlockSpec((1, D), lambda i, j: (0, 0)),
        pl.BlockSpec((None, D, tn), lambda i, j: (layer, 0, j)),
    ]
    args = [x, nw.reshape(1, D), w]
    out_shape = [jax.ShapeDtypeStruct((M, N), BF16)]
    out_specs = [pl.BlockSpec((tm, tn), lambda i, j: (i, j))]
    if has_extra:
        ne = w_extra.shape[1]
        in_specs.append(pl.BlockSpec((D, ne), lambda i, j: (0, 0)))
        args.append(w_extra)
        out_shape.append(jax.ShapeDtypeStruct((M, ne), BF16))
        out_specs.append(pl.BlockSpec((tm, ne), lambda i, j: (i, 0)))
    res = pl.pallas_call(
        functools.partial(_norm_proj_kernel, has_extra=has_extra),
        grid=(M // tm, N // tn),
        in_specs=in_specs,
        out_specs=out_specs,
        out_shape=out_shape,
        scratch_shapes=[pltpu.VMEM((tm, D), BF16)],
        compiler_params=_params(("parallel", "arbitrary")),
        name=name,
    )(*args)
    return res if has_extra else res[0]


def _proj_res_kernel(y_ref, w_ref, x_ref, o_ref):
    o_ref[...] = x_ref[...] + _dot(y_ref[...], w_ref[...])


def _proj_res(y, w, layer, x, *, tm, tn, name):
    M, K = y.shape
    N = w.shape[2]
    return pl.pallas_call(
        _proj_res_kernel,
        grid=(M // tm, N // tn),
        in_specs=[
            pl.BlockSpec((tm, K), lambda i, j: (i, 0)),
            pl.BlockSpec((None, K, tn), lambda i, j: (layer, 0, j)),
            pl.BlockSpec((tm, tn), lambda i, j: (i, j)),
        ],
        out_specs=pl.BlockSpec((tm, tn), lambda i, j: (i, j)),
        out_shape=jax.ShapeDtypeStruct((M, N), F32),
        compiler_params=_params(("parallel", "arbitrary")),
        name=name,
    )(y, w, x)


def _interleave(*lists):
    tagged = []
    for li, steps in enumerate(lists):
        tagged += [((k + 0.5) / len(steps), li, f) for k, f in enumerate(steps)]
    tagged.sort(key=lambda t: t[:2])
    return [f for _, _, f in tagged]


def _ffn_kernel(x_ref, nw_ref, wg_ref, wv_ref, cwg_ref, cwv_ref, cbg_ref, cbv_ref, wd_ref, *rest,
                tiles_per_seq, final_norm, cast_next, sub):
    rest = list(rest)
    fnw_ref = rest.pop(0) if final_norm else None
    if cast_next:
        wun_ref, wdn_ref = rest.pop(0), rest.pop(0)
        o_ref, wun_o_ref, wdn_o_ref = rest.pop(0), rest.pop(0), rest.pop(0)
        wun_o_ref[...] = wun_ref[...].astype(BF16)
        wdn_o_ref[...] = wdn_ref[...].astype(BF16)
    else:
        o_ref = rest.pop(0)
    h_ref, carg_ref, carv_ref, ug_ref, uv_ref, act_ref = rest
    i = pl.program_id(0)
    j = pl.program_id(1)
    nj = pl.num_programs(1)
    tm, dm = x_ref.shape
    tf = wd_ref.shape[0]
    nsub = tf // sub
    pad = SUBLANES

    @pl.when(j == 0)
    def _():
        x = x_ref[...]
        ms = jnp.mean(x * x, axis=-1, keepdims=True)
        h_ref[...] = (x * lax.rsqrt(ms + RMS_EPS) * nw_ref[...]).astype(BF16)
        o_ref[...] = x

    seq_start = (i % tiles_per_seq) == 0

    def up_steps(s):
        p = s % 2
        cs = pl.ds(s * sub, sub)
        acc = {}

        def kdot(w_ref, key, k):
            def f():
                ks = pl.ds(k * MXU_TILE, MXU_TILE)
                d = _dot(h_ref[:, ks], w_ref[ks, cs])
                acc[key] = d if k == 0 else acc[key] + d
            return f

        def park(key, u_ref, car_ref):
            def f():
                u = acc[key]
                u_ref[p, 0:pad, :] = jnp.where(seq_start, 0.0, car_ref[j, :, cs])
                car_ref[j, :, cs] = u[tm - pad:, :]
                u_ref[p, pad:pad + tm, :] = u
            return f

        steps = [kdot(wg_ref, "g", k) for k in range(dm // MXU_TILE)] + [park("g", ug_ref, carg_ref)]
        steps += [kdot(wv_ref, "v", k) for k in range(dm // MXU_TILE)] + [park("v", uv_ref, carv_ref)]
        return steps

    def gate_steps(s):
        p = s % 2
        cs = pl.ds(s * sub, sub)

        def conv(u_ref, cw_ref, cb_ref, r0):
            cw = cw_ref[:, cs]
            kw = cw.shape[0]
            y = cb_ref[:, cs] + cw[kw - 1:kw, :] * u_ref[p, pad + r0:pad + r0 + ROW_BLK, :]
            for d in range(1, kw):
                y = y + cw[kw - 1 - d:kw - d, :] * u_ref[p, pad + r0 - d:pad + r0 - d + ROW_BLK, :]
            return y

        def block(r0):
            def f():
                gate = conv(ug_ref, cwg_ref, cbg_ref, r0)
                val = conv(uv_ref, cwv_ref, cbv_ref, r0)
                act_ref[p, r0:r0 + ROW_BLK, :] = (_gelu_tanh(gate) * val).astype(BF16)
            return f

        return [block(r0) for r0 in range(0, tm, ROW_BLK)]

    def down_steps(s):
        p = s % 2
        cs = pl.ds(s * sub, sub)

        def ndot(n):
            def f():
                ns = pl.ds(n * MXU_TILE, MXU_TILE)
                o_ref[:, ns] += _dot(act_ref[p], wd_ref[cs, ns])
            return f

        return [ndot(n) for n in range(dm // MXU_TILE)]

    for f in up_steps(0):
        f()
    for s in range(nsub):
        slot = _interleave(gate_steps(s),
                           up_steps(s + 1) if s + 1 < nsub else [],
                           down_steps(s - 1) if s > 0 else [])
        for f in slot:
            f()
    for f in down_steps(nsub - 1):
        f()

    if final_norm:
        @pl.when(j == nj - 1)
        def _():
            y = o_ref[...]
            ms = jnp.mean(y * y, axis=-1, keepdims=True)
            o_ref[...] = y * lax.rsqrt(ms + RMS_EPS) * fnw_ref[...]


def _ffn(x, nw, w_up, conv_w, conv_b, w_down, layer, final_nw, next_f32, *, seq_len, tm, tf, sub, name):
    M, D = x.shape
    F = w_down.shape[1]
    nj = F // tf
    ni = M // tm
    final_norm = final_nw is not None
    cast_next = next_f32 is not None
    in_specs = [
        pl.BlockSpec((tm, D), lambda i, j: (i, 0)),
        pl.BlockSpec((1, D), lambda i, j: (0, 0)),
        pl.BlockSpec((None, D, tf), lambda i, j: (layer, 0, j)),
        pl.BlockSpec((None, D, tf), lambda i, j: (layer, 0, nj + j)),
        pl.BlockSpec((conv_w.shape[0], tf), lambda i, j: (0, j)),
        pl.BlockSpec((conv_w.shape[0], tf), lambda i, j: (0, nj + j)),
        pl.BlockSpec((1, tf), lambda i, j: (0, j)),
        pl.BlockSpec((1, tf), lambda i, j: (0, nj + j)),
        pl.BlockSpec((None, tf, D), lambda i, j: (layer, j, 0)),
    ]
    cb = conv_b.reshape(1, 2 * F)
    args = [x, nw.reshape(1, D), w_up, w_up, conv_w, conv_w, cb, cb, w_down]
    if final_norm:
        in_specs.append(pl.BlockSpec((1, D), lambda i, j: (0, 0)))
        args.append(final_nw.reshape(1, D))
    out_specs = [pl.BlockSpec((tm, D), lambda i, j: (i, 0))]
    out_shape = [jax.ShapeDtypeStruct((M, D), F32)]
    if cast_next:
        nxt_up, nxt_down, nxt = next_f32
        up_blk = (None, D // ni, 2 * F // nj)
        dn_blk = (None, F // (ni * nj), D)
        in_specs += [pl.BlockSpec(up_blk, lambda i, j: (nxt, i, j)),
                     pl.BlockSpec(dn_blk, lambda i, j: (nxt, i * nj + j, 0))]
        args += [nxt_up, nxt_down]
        out_specs += [pl.BlockSpec(up_blk, lambda i, j: (0, i, j)),
                      pl.BlockSpec(dn_blk, lambda i, j: (0, i * nj + j, 0))]
        out_shape += [jax.ShapeDtypeStruct((1, D, 2 * F), BF16), jax.ShapeDtypeStruct((1, F, D), BF16)]
    res = pl.pallas_call(
        functools.partial(_ffn_kernel, tiles_per_seq=seq_len // tm, final_norm=final_norm,
                          cast_next=cast_next, sub=sub),
        grid=(ni, nj),
        in_specs=in_specs,
        out_specs=out_specs,
        out_shape=out_shape,
        scratch_shapes=[
            pltpu.VMEM((tm, D), BF16),
            pltpu.VMEM((nj, SUBLANES, tf), F32),
            pltpu.VMEM((nj, SUBLANES, tf), F32),
            pltpu.VMEM((2, SUBLANES + tm, sub), F32),
            pltpu.VMEM((2, SUBLANES + tm, sub), F32),
            pltpu.VMEM((2, tm, sub), BF16),
        ],
        compiler_params=_params(("arbitrary", "arbitrary")),
        name=name,
    )(*args)
    return res if cast_next else res[0]


def _tile_masks(tc):
    r = lax.broadcasted_iota(jnp.int32, (tc, tc), 0)
    c = lax.broadcasted_iota(jnp.int32, (tc, tc), 1)
    shift = CHUNK.bit_length() - 1
    same = lax.shift_right_logical(r, shift) == lax.shift_right_logical(c, shift)
    return r, c, same


def _gla_kernel(q_ref, k_ref, v_ref, g_ref, z_ref, wgk_ref, bgk_ref, nw_ref, o_ref, st_ref, *,
                n_chunks, q_scale):
    @pl.when(pl.program_id(1) == 0)
    def _():
        st_ref[...] = jnp.zeros_like(st_ref)

    nb, tc, dk = q_ref.shape
    r, c, same = _tile_masks(tc)
    lower = same & (r >= c)
    upper = same & (r < c)
    tri = lower.astype(BF16)
    chunk_rows = [slice(ci * CHUNK, (ci + 1) * CHUNK) for ci in range(n_chunks)]

    def chain(bi):
        t = {}

        def gate():
            pre = _dot(z_ref[bi], wgk_ref[...]) + bgk_ref[...]
            log_a = -_softplus(-pre) / GLA_GATE_TAU
            t["hi"] = log_a.astype(BF16)
            t["lo"] = (log_a - t["hi"].astype(F32)).astype(BF16)

        def cumsum():
            b = _dot(tri, t["hi"]) + _dot(tri, t["lo"])
            t["b"] = b
            t["b_last"] = [b[(ci + 1) * CHUNK - 1:(ci + 1) * CHUNK, :] for ci in range(n_chunks)]

        def decays():
            b = t["b"]
            b_last_rows = jnp.concatenate(
                [jnp.broadcast_to(bl, (CHUNK, dk)) for bl in t["b_last"]], axis=0)
            eb = jnp.exp(b)
            enb = jnp.exp(-b)
            q = q_ref[bi].astype(F32) * q_scale
            k = k_ref[bi].astype(F32)
            t["q_dec"] = (q * eb).astype(BF16)
            t["k_grow"] = (k * enb).astype(BF16)
            t["q_grow"] = (q * enb).astype(BF16)
            t["k_dec"] = (k * eb).astype(BF16)
            t["k_in"] = (k * jnp.exp(b_last_rows - b)).astype(BF16)

        def intra():
            scores = jnp.where(lower, _dot_nt(t["q_dec"], t["k_grow"]),
                               jnp.where(upper, _dot_nt(t["q_grow"], t["k_dec"]), 0.0))
            t["o_intra"] = _dot(scores.astype(BF16), v_ref[bi])

        def state():
            v = v_ref[bi]
            incs = [_dot_tn(v[rows, :], t["k_in"][rows, :]) for rows in chunk_rows]
            states = [st_ref[bi]]
            for ci in range(n_chunks):
                states.append(jnp.exp(t["b_last"][ci]) * states[ci] + incs[ci])
            st_ref[bi] = states[n_chunks]
            t["states"] = states

        def readout():
            o = jnp.concatenate(
                [t["o_intra"][rows, :] + _dot_nt(t["q_dec"][rows, :], t["states"][ci].astype(BF16))
                 for ci, rows in enumerate(chunk_rows)], axis=0)
            o = o * lax.rsqrt(jnp.mean(o * o, axis=-1, keepdims=True) + RMS_EPS) * nw_ref[...]
            g = g_ref[bi].astype(F32)
            o_ref[bi] = (o * (g * _sigmoid(g))).astype(o_ref.dtype)

        return [gate, cumsum, decays, intra, state, readout]

    for f in _interleave(*[chain(bi) for bi in range(nb)]):
        f()


def _gla_core(proj, z, wgk, bgk, norm_w, *, batch, seq_len, tc, name):
    M, n_proj = proj.shape
    kd = wgk.shape[1]
    vd = norm_w.shape[0]
    H = GLA_HEADS
    dk, dv = kd // H, vd // H
    nt = seq_len // tc
    proj3 = proj.reshape(batch, seq_len, n_proj)
    y = pl.pallas_call(
        functools.partial(_gla_kernel, n_chunks=tc // CHUNK, q_scale=dk ** -0.5),
        grid=(H, nt),
        in_specs=[
            pl.BlockSpec((batch, tc, dk), lambda h, n: (0, n, h)),
            pl.BlockSpec((batch, tc, dk), lambda h, n: (0, n, kd // dk + h)),
            pl.BlockSpec((batch, tc, dv), lambda h, n: (0, n, 2 * kd // dv + h)),
            pl.BlockSpec((batch, tc, dv), lambda h, n: (0, n, (2 * kd + vd) // dv + h)),
            pl.BlockSpec((batch, tc, z.shape[1]), lambda h, n: (0, n, 0)),
            pl.BlockSpec((wgk.shape[0], dk), lambda h, n: (0, h)),
            pl.BlockSpec((1, dk), lambda h, n: (0, h)),
            pl.BlockSpec((1, dv), lambda h, n: (0, h)),
        ],
        out_specs=pl.BlockSpec((batch, tc, dv), lambda h, n: (0, n, h)),
        out_shape=jax.ShapeDtypeStruct((batch, seq_len, vd), BF16),
        scratch_shapes=[pltpu.VMEM((batch, dv, dk), F32)],
        compiler_params=_params(("arbitrary", "arbitrary")),
        name=name,
    )(proj3, proj3, proj3, proj3, z.reshape(batch, seq_len, z.shape[1]), wgk,
      bgk.reshape(1, kd), norm_w.reshape(1, vd))
    return y.reshape(M, vd)


def _gla_layer_kernel(x_ref, nw_ref, w_ref, wz_ref, wgk_ref, bgk_ref, gnw_ref, o_ref,
                      h_ref, st_ref, z_ref, q_ref, k_ref, v_ref, g_ref, *, heads, q_scale):
    nb, tc, dm = x_ref.shape
    kd = wgk_ref.shape[1]
    vd = o_ref.shape[2]
    dk, dv = kd // heads, vd // heads
    n_chunks = tc // CHUNK

    @pl.when(pl.program_id(0) == 0)
    def _():
        st_ref[...] = jnp.zeros_like(st_ref)

    for bi in range(nb):
        x = x_ref[bi]
        ms = jnp.mean(x * x, axis=-1, keepdims=True)
        h_ref[bi * tc:(bi + 1) * tc, :] = (x * lax.rsqrt(ms + RMS_EPS) * nw_ref[...]).astype(BF16)
    z_ref[...] = _dot(h_ref[...], wz_ref[...]).astype(BF16)

    r, c, same = _tile_masks(tc)
    lower = same & (r >= c)
    upper = same & (r < c)
    tri = lower.astype(BF16)
    chunk_rows = [slice(ci * CHUNK, (ci + 1) * CHUNK) for ci in range(n_chunks)]

    def proj_steps(hd):
        p = hd % 2
        steps = []
        for dst, col0, width in ((q_ref, hd * dk, dk), (k_ref, kd + hd * dk, dk),
                                 (v_ref, 2 * kd + hd * dv, dv), (g_ref, 2 * kd + vd + hd * dv, dv)):
            for c0 in range(0, width, MXU_TILE):
                acc = {}

                def kdot(kb, col=col0 + c0, acc=acc):
                    def f():
                        ks = pl.ds(kb * MXU_TILE, MXU_TILE)
                        d = _dot(h_ref[:, ks], w_ref[ks, pl.ds(col, MXU_TILE)])
                        acc["v"] = d if kb == 0 else acc["v"] + d
                    return f

                def park(dst=dst, c0=c0, acc=acc):
                    def f():
                        dst[p, :, c0:c0 + MXU_TILE] = acc["v"].astype(dst.dtype)
                    return f

                steps += [kdot(kb) for kb in range(dm // MXU_TILE)] + [park()]
        return steps

    def chain(hd, bi):
        p = hd % 2
        rows_b = slice(bi * tc, (bi + 1) * tc)
        kcols = slice(hd * dk, (hd + 1) * dk)
        vcols = slice(hd * dv, (hd + 1) * dv)
        t = {}

        def gate():
            pre = _dot(z_ref[rows_b, :], wgk_ref[:, kcols]) + bgk_ref[:, kcols]
            log_a = -_softplus(-pre) / GLA_GATE_TAU
            t["hi"] = log_a.astype(BF16)
            t["lo"] = (log_a - t["hi"].astype(F32)).astype(BF16)

        def cumsum():
            b = _dot(tri, t["hi"]) + _dot(tri, t["lo"])
            t["b"] = b
            t["b_last"] = [b[(ci + 1) * CHUNK - 1:(ci + 1) * CHUNK, :] for ci in range(n_chunks)]

        def decays():
            b = t["b"]
            b_last_rows = jnp.concatenate(
                [jnp.broadcast_to(bl, (CHUNK, dk)) for bl in t["b_last"]], axis=0)
            eb = jnp.exp(b)
            enb = jnp.exp(-b)
            q = q_ref[p, rows_b, :] * q_scale
            k = k_ref[p, rows_b, :]
            t["q_dec"] = (q * eb).astype(BF16)
            t["k_grow"] = (k * enb).astype(BF16)
            t["q_grow"] = (q * enb).astype(BF16)
            t["k_dec"] = (k * eb).astype(BF16)
            t["k_in"] = (k * jnp.exp(b_last_rows - b)).astype(BF16)

        def intra():
            scores = jnp.where(lower, _dot_nt(t["q_dec"], t["k_grow"]),
                               jnp.where(upper, _dot_nt(t["q_grow"], t["k_dec"]), 0.0))
            t["o_intra"] = _dot(scores.astype(BF16), v_ref[p, rows_b, :])

        def state():
            v = v_ref[p, rows_b, :]
            incs = [_dot_tn(v[rows, :], t["k_in"][rows, :]) for rows in chunk_rows]
            states = [st_ref[bi, hd]]
            for ci in range(n_chunks):
                states.append(jnp.exp(t["b_last"][ci]) * states[ci] + incs[ci])
            st_ref[bi, hd] = states[n_chunks]
            t["states"] = states

        def readout():
            o = jnp.concatenate(
                [t["o_intra"][rows, :] + _dot_nt(t["q_dec"][rows, :], t["states"][ci].astype(BF16))
                 for ci, rows in enumerate(chunk_rows)], axis=0)
            o = o * lax.rsqrt(jnp.mean(o * o, axis=-1, keepdims=True) + RMS_EPS) * gnw_ref[:, vcols]
            g = g_ref[p, rows_b, :]
            o_ref[bi, :, vcols] = (o * (g * _sigmoid(g))).astype(o_ref.dtype)

        return [gate, cumsum, decays, intra, state, readout]

    for f in proj_steps(0):
        f()
    for hd in range(heads):
        slot = [chain(hd, bi) for bi in range(nb)]
        if hd + 1 < heads:
            slot.append(proj_steps(hd + 1))
        for f in _interleave(*slot):
            f()


def _gla_layer(x3, nw, w_in, layer, wgk, bgk, gnorm_w, *, tc, name):
    B, S, D = x3.shape
    kd = wgk.shape[1]
    vd = gnorm_w.shape[0]
    H = GLA_HEADS
    dk, dv = kd // H, vd // H
    rows = B * tc
    const = lambda n: (0, 0)
    return pl.pallas_call(
        functools.partial(_gla_layer_kernel, heads=H, q_scale=dk ** -0.5),
        grid=(S // tc,),
        in_specs=[
            pl.BlockSpec((B, tc, D), lambda n: (0, n, 0)),
            pl.BlockSpec((1, D), const),
            pl.BlockSpec((None, D, 2 * kd + 2 * vd), lambda n: (layer, 0, 0),
                         pipeline_mode=pl.Buffered(1)),
            pl.BlockSpec((None, D, LANES), lambda n: (layer, 0, (2 * kd + 2 * vd) // LANES)),
            pl.BlockSpec(wgk.shape, const),
            pl.BlockSpec((1, kd), const),
            pl.BlockSpec((1, vd), const),
        ],
        out_specs=pl.BlockSpec((B, tc, vd), lambda n: (0, n, 0)),
        out_shape=jax.ShapeDtypeStruct((B, S, vd), BF16),
        scratch_shapes=[
            pltpu.VMEM((rows, D), BF16),
            pltpu.VMEM((B, H, dv, dk), F32),
            pltpu.VMEM((rows, LANES), BF16),
            pltpu.VMEM((2, rows, dk), F32),
            pltpu.VMEM((2, rows, dk), F32),
            pltpu.VMEM((2, rows, dv), BF16),
            pltpu.VMEM((2, rows, dv), F32),
        ],
        compiler_params=_params(("arbitrary",)),
        name=name,
    )(x3, nw.reshape(1, D), w_in, w_in, wgk, bgk.reshape(1, kd), gnorm_w.reshape(1, vd))


def _rope_kernel(cos_ref, sin_ref):
    tt, half = cos_ref.shape
    pos = (pl.program_id(0) * tt + lax.broadcasted_iota(jnp.int32, (tt, half), 0)).astype(F32)
    jj = lax.broadcasted_iota(jnp.int32, (tt, half), 1).astype(F32)
    inv = jnp.exp(-(jj / half) * math.log(ROPE_BASE))
    ang = pos * inv
    cos_ref[...] = jnp.cos(ang)
    sin_ref[...] = jnp.sin(ang)


def _rope_tables(seq_len, half, *, tt):
    return pl.pallas_call(
        _rope_kernel,
        grid=(seq_len // tt,),
        out_specs=[pl.BlockSpec((tt, half), lambda i: (i, 0))] * 2,
        out_shape=[jax.ShapeDtypeStruct((seq_len, half), F32)] * 2,
        compiler_params=_params(("parallel",)),
        name="rope_tables",
    )()


def _ret_layer_kernel(x_ref, nw_ref, wq_ref, wk_ref, wv_ref, wg_ref, cos_ref, sin_ref, rnw_ref, o_ref,
                      h_ref, st_ref, dmat_ref, xi_ref, zeta_ref, q_ref, k_ref, v_ref, g_ref, *,
                      heads_per_step, k_scale):
    nb, tc, dm = x_ref.shape
    dk = wq_ref.shape[1] // heads_per_step
    dv = wv_ref.shape[1] // heads_per_step
    half = dk // 2
    n = pl.program_id(0)
    hp = pl.program_id(1)

    def log_gamma(hd, shape):
        return jnp.log(1.0 - jnp.exp2(-5.0 - jnp.full(shape, hd.astype(F32), F32)))

    @pl.when(hp == 0)
    def _():
        for bi in range(nb):
            x = x_ref[bi]
            ms = jnp.mean(x * x, axis=-1, keepdims=True)
            h_ref[bi * tc:(bi + 1) * tc, :] = (x * lax.rsqrt(ms + RMS_EPS) * nw_ref[...]).astype(BF16)

    @pl.when(n == 0)
    def _():
        r, c, _ = _tile_masks(tc)
        shift = CHUNK.bit_length() - 1
        reads = lax.shift_right_logical(c, shift) <= lax.shift_right_logical(r, shift)
        dist = jnp.abs(r - c).astype(F32)
        pos = lax.broadcasted_iota(jnp.int32, (tc, dk), 0).astype(F32)
        for hh in range(heads_per_step):
            hd = hp * heads_per_step + hh
            for bi in range(nb):
                st_ref[bi, hd] = jnp.zeros((dv, dk), F32)
            dmat_ref[hd] = jnp.where(reads, jnp.exp(log_gamma(hd, (tc, tc)) * dist), 0.0)
            lg = log_gamma(hd, (tc, dk))
            xi_ref[hd] = jnp.exp(lg * (pos + 1.0))
            zeta_ref[hd] = jnp.exp(lg * (tc - 1.0 - pos))

    def proj_steps(hh):
        steps = []
        for w_ref, dst, width in ((wq_ref, q_ref, dk), (wk_ref, k_ref, dk), (wv_ref, v_ref, dv),
                                  (wg_ref, g_ref, dv)):
            for c0 in range(0, width, MXU_TILE):
                acc = {}

                def kdot(kb, w_ref=w_ref, col=hh * width + c0, acc=acc):
                    def f():
                        ks = pl.ds(kb * MXU_TILE, MXU_TILE)
                        d = _dot(h_ref[:, ks], w_ref[ks, col:col + MXU_TILE])
                        acc["v"] = d if kb == 0 else acc["v"] + d
                    return f

                def park(dst=dst, c0=c0, acc=acc):
                    def f():
                        dst[hh, :, c0:c0 + MXU_TILE] = acc["v"].astype(dst.dtype)
                    return f

                steps += [kdot(kb) for kb in range(dm // MXU_TILE)] + [park()]
        return steps

    def rope(x):
        x1, x2 = x[:, :half], x[:, half:]
        cos, sin = cos_ref[...], sin_ref[...]
        return jnp.concatenate([x1 * cos - x2 * sin, x2 * cos + x1 * sin], axis=-1)

    def chain(hh, bi):
        hd = hp * heads_per_step + hh
        rows = slice(bi * tc, (bi + 1) * tc)
        t = {}

        def rotate():
            t["q"] = rope(q_ref[hh, rows, :])
            t["k"] = rope(k_ref[hh, rows, :]) * k_scale

        def scores():
            t["s"] = (_dot_nt(t["q"].astype(BF16), t["k"].astype(BF16)) * dmat_ref[hd]).astype(BF16)

        def mix():
            st = st_ref[bi, hd]
            v = v_ref[hh, rows, :]
            t["o"] = _dot(t["s"], v) + _dot_nt((t["q"] * xi_ref[hd]).astype(BF16), st.astype(BF16))
            decay = jnp.exp(log_gamma(hd, (1, dk)) * float(tc))
            st_ref[bi, hd] = decay * st + _dot_tn(v, (t["k"] * zeta_ref[hd]).astype(BF16))

        def finish():
            o = t["o"]
            oc = o - jnp.mean(o, axis=-1, keepdims=True)
            o = (oc * lax.rsqrt(jnp.mean(oc * oc, axis=-1, keepdims=True) + RMS_EPS)
                 * rnw_ref[:, hh * dv:(hh + 1) * dv])
            g = g_ref[hh, rows, :]
            o_ref[bi, :, hh * dv:(hh + 1) * dv] = (o * (g * _sigmoid(g))).astype(o_ref.dtype)

        return [rotate, scores, mix, finish]

    for f in proj_steps(0):
        f()
    for hh in range(heads_per_step):
        slot = [chain(hh, bi) for bi in range(nb)]
        if hh + 1 < heads_per_step:
            slot.append(proj_steps(hh + 1))
        for f in _interleave(*slot):
            f()


def _ret_layer(x3, nw, w_in, layer, cos, sin, norm_w, *, tc, heads_per_step, name):
    B, S, D = x3.shape
    vd = norm_w.shape[0]
    H = RET_HEADS
    dv = vd // H
    dk = cos.shape[1] * 2
    kd = H * dk
    rows = B * tc
    hs = heads_per_step
    np_ = H // hs
    return pl.pallas_call(
        functools.partial(_ret_layer_kernel, heads_per_step=hs, k_scale=dk ** -0.5),
        grid=(S // tc, np_),
        in_specs=[
            pl.BlockSpec((B, tc, D), lambda n, p: (0, n, 0)),
            pl.BlockSpec((1, D), lambda n, p: (0, 0)),
            pl.BlockSpec((None, D, hs * dk), lambda n, p: (layer, 0, p)),
            pl.BlockSpec((None, D, hs * dk), lambda n, p: (layer, 0, np_ + p)),
            pl.BlockSpec((None, D, hs * dv), lambda n, p: (layer, 0, 2 * kd // (hs * dv) + p)),
            pl.BlockSpec((None, D, hs * dv), lambda n, p: (layer, 0, (2 * kd + vd) // (hs * dv) + p)),
            pl.BlockSpec((tc, dk // 2), lambda n, p: (n, 0)),
            pl.BlockSpec((tc, dk // 2), lambda n, p: (n, 0)),
            pl.BlockSpec((1, hs * dv), lambda n, p: (0, p)),
        ],
        out_specs=pl.BlockSpec((B, tc, hs * dv), lambda n, p: (0, n, p)),
        out_shape=jax.ShapeDtypeStruct((B, S, vd), BF16),
        scratch_shapes=[
            pltpu.VMEM((rows, D), BF16),
            pltpu.VMEM((B, H, dv, dk), F32),
            pltpu.VMEM((H, tc, tc), F32),
            pltpu.VMEM((H, tc, dk), F32),
            pltpu.VMEM((H, tc, dk), F32),
            pltpu.VMEM((hs, rows, dk), F32),
            pltpu.VMEM((hs, rows, dk), F32),
            pltpu.VMEM((hs, rows, dv), BF16),
            pltpu.VMEM((hs, rows, dv), F32),
        ],
        compiler_params=_params(("arbitrary", "arbitrary")),
        name=name,
    )(x3, nw.reshape(1, D), w_in, w_in, w_in, w_in, cos, sin, norm_w.reshape(1, vd))


def _ret_kernel(q_ref, k_ref, v_ref, g_ref, cos_ref, sin_ref, nw_ref, o_ref,
                st_ref, dmat_ref, xi_ref, zeta_ref, *, k_scale):
    nb, tc, dk = q_ref.shape
    half = dk // 2
    head = pl.program_id(0).astype(F32)

    def log_gamma(shape):
        return jnp.log(1.0 - jnp.exp2(-5.0 - jnp.full(shape, head, F32)))

    @pl.when(pl.program_id(1) == 0)
    def _():
        st_ref[...] = jnp.zeros_like(st_ref)
        r, c, _ = _tile_masks(tc)
        shift = CHUNK.bit_length() - 1
        reads = lax.shift_right_logical(c, shift) <= lax.shift_right_logical(r, shift)
        dist = jnp.abs(r - c).astype(F32)
        dmat_ref[...] = jnp.where(reads, jnp.exp(log_gamma((tc, tc)) * dist), 0.0)
        pos = lax.broadcasted_iota(jnp.int32, (tc, dk), 0).astype(F32)
        lg = log_gamma((tc, dk))
        xi_ref[...] = jnp.exp(lg * (pos + 1.0))
        zeta_ref[...] = jnp.exp(lg * (tc - 1.0 - pos))

    decay = jnp.exp(log_gamma((1, dk)) * float(tc))

    def rope(x):
        x1, x2 = x[:, :half], x[:, half:]
        cos, sin = cos_ref[...], sin_ref[...]
        return jnp.concatenate([x1 * cos - x2 * sin, x2 * cos + x1 * sin], axis=-1)

    def chain(bi):
        t = {}

        def rotate():
            t["q"] = rope(q_ref[bi].astype(F32))
            t["k"] = rope(k_ref[bi].astype(F32)) * k_scale

        def scores():
            t["s"] = (_dot_nt(t["q"].astype(BF16), t["k"].astype(BF16)) * dmat_ref[...]).astype(BF16)

        def mix():
            st = st_ref[bi]
            v = v_ref[bi]
            t["o"] = _dot(t["s"], v) + _dot_nt((t["q"] * xi_ref[...]).astype(BF16), st.astype(BF16))
            st_ref[bi] = decay * st + _dot_tn(v, (t["k"] * zeta_ref[...]).astype(BF16))

        def finish():
            o = t["o"]
            oc = o - jnp.mean(o, axis=-1, keepdims=True)
            o = oc * lax.rsqrt(jnp.mean(oc * oc, axis=-1, keepdims=True) + RMS_EPS) * nw_ref[...]
            g = g_ref[bi].astype(F32)
            o_ref[bi] = (o * (g * _sigmoid(g))).astype(o_ref.dtype)

        return [rotate, scores, mix, finish]

    for f in _interleave(*[chain(bi) for bi in range(nb)]):
        f()


def _ret_core(proj, cos, sin, norm_w, *, batch, seq_len, tc, name):
    M, n_proj = proj.shape
    vd = norm_w.shape[0]
    H = RET_HEADS
    dv = vd // H
    dk = cos.shape[1] * 2
    kd = H * dk
    proj3 = proj.reshape(batch, seq_len, n_proj)
    y = pl.pallas_call(
        functools.partial(_ret_kernel, k_scale=dk ** -0.5),
        grid=(H, seq_len // tc),
        in_specs=[
            pl.BlockSpec((batch, tc, dk), lambda h, n: (0, n, h)),
            pl.BlockSpec((batch, tc, dk), lambda h, n: (0, n, H + h)),
            pl.BlockSpec((batch, tc, dv), lambda h, n: (0, n, 2 * kd // dv + h)),
            pl.BlockSpec((batch, tc, dv), lambda h, n: (0, n, (2 * kd + vd) // dv + h)),
            pl.BlockSpec((tc, dk // 2), lambda h, n: (n, 0)),
            pl.BlockSpec((tc, dk // 2), lambda h, n: (n, 0)),
            pl.BlockSpec((1, dv), lambda h, n: (0, h)),
        ],
        out_specs=pl.BlockSpec((batch, tc, dv), lambda h, n: (0, n, h)),
        out_shape=jax.ShapeDtypeStruct((batch, seq_len, vd), BF16),
        scratch_shapes=[
            pltpu.VMEM((batch, dv, dk), F32),
            pltpu.VMEM((tc, tc), F32),
            pltpu.VMEM((tc, dk), F32),
            pltpu.VMEM((tc, dk), F32),
        ],
        compiler_params=_params(("arbitrary", "arbitrary")),
        name=name,
    )(proj3, proj3, proj3, proj3, cos, sin, norm_w.reshape(1, vd))
    return y.reshape(M, vd)


def _lru_kernel(xb_ref, yb_ref, cw_ref, cb_ref, wga_ref, bga_ref, wgx_ref, bgx_ref, lam_ref, o_ref,
                hcar_ref, ccar_ref, a_ref, u_ref, hs_ref, *, n_blocks):
    tt, width = xb_ref.shape
    kw = cw_ref.shape[0]

    @pl.when(pl.program_id(1) == 0)
    def _():
        hcar_ref[...] = jnp.zeros_like(hcar_ref)
        ccar_ref[...] = jnp.zeros_like(ccar_ref)

    bw = width // n_blocks
    for nb in range(n_blocks):
        cs = pl.ds(nb * bw, bw)
        xb = xb_ref[:, cs].astype(F32)
        ext = jnp.concatenate([ccar_ref[:, cs], xb], axis=0)
        ccar_ref[:, cs] = xb[tt - SUBLANES:, :]
        cw = cw_ref[:, cs]
        xc = cw[kw - 1:kw, :] * xb + cb_ref[:, cs]
        for d in range(1, kw):
            xc = xc + cw[kw - 1 - d:kw - d, :] * ext[SUBLANES - d:SUBLANES - d + tt, :]
        xcb = xc.astype(BF16)
        rg = _sigmoid(_dot(xcb, wga_ref[nb]) + bga_ref[:, cs])
        ig = _sigmoid(_dot(xcb, wgx_ref[nb]) + bgx_ref[:, cs])
        a = jnp.exp(rg * (-LRU_C * _softplus(-lam_ref[:, cs])))
        a_ref[:, cs] = a
        u_ref[:, cs] = xc * ig * jnp.sqrt(1.0 - a * a)

    def step(t, h):
        h = a_ref[pl.ds(t, 1), :] * h + u_ref[pl.ds(t, 1), :]
        hs_ref[pl.ds(t, 1), :] = h
        return h

    hcar_ref[0:1, :] = lax.fori_loop(0, tt, step, hcar_ref[0:1, :], unroll=8)

    for nb in range(n_blocks):
        cs = pl.ds(nb * bw, bw)
        o_ref[:, cs] = (hs_ref[:, cs] * _gelu_tanh(yb_ref[:, cs].astype(F32))).astype(o_ref.dtype)


def _lru_core(proj, conv_w, conv_b, w_ga, b_ga, w_gx, b_gx, lam, layer, *, batch, seq_len, tt, name):
    M = proj.shape[0]
    width = conv_w.shape[1]
    nt = seq_len // tt
    nb = w_ga.shape[1]
    vec = lambda: pl.BlockSpec((1, width), lambda b, n: (0, 0))
    wblk = lambda: pl.BlockSpec((None,) + w_ga.shape[1:], lambda b, n: (layer, 0, 0, 0))
    return pl.pallas_call(
        functools.partial(_lru_kernel, n_blocks=nb),
        grid=(batch, nt),
        in_specs=[
            pl.BlockSpec((tt, width), lambda b, n: (b * nt + n, 0)),
            pl.BlockSpec((tt, width), lambda b, n: (b * nt + n, 1)),
            pl.BlockSpec(conv_w.shape, lambda b, n: (0, 0)),
            vec(), wblk(), vec(), wblk(), vec(), vec(),
        ],
        out_specs=pl.BlockSpec((tt, width), lambda b, n: (b * nt + n, 0)),
        out_shape=jax.ShapeDtypeStruct((M, width), BF16),
        scratch_shapes=[
            pltpu.VMEM((SUBLANES, width), F32),
            pltpu.VMEM((SUBLANES, width), F32),
            pltpu.VMEM((tt, width), F32),
            pltpu.VMEM((tt, width), F32),
            pltpu.VMEM((tt, width), F32),
        ],
        compiler_params=_params(("arbitrary", "arbitrary")),
        name=name,
    )(proj, proj, conv_w, conv_b.reshape(1, width), w_ga, b_ga.reshape(1, width),
      w_gx, b_gx.reshape(1, width), lam.reshape(1, width))


def _lru_layer_kernel(x_ref, nw_ref, w_ref, cw_ref, cb_ref, wga_ref, bga_ref, wgx_ref, bgx_ref,
                      lam_ref, o_ref, h_ref, hcar_ref, ccar_ref, xb_ref, yb_ref, a_ref, u_ref, hs_ref,
                      gy_ref, *, n_blocks):
    tt, dm = x_ref.shape
    width = o_ref.shape[1]
    kw = cw_ref.shape[0]
    bw = width // n_blocks

    @pl.when(pl.program_id(1) == 0)
    def _():
        hcar_ref[...] = jnp.zeros_like(hcar_ref)
        ccar_ref[...] = jnp.zeros_like(ccar_ref)

    x = x_ref[...]
    ms = jnp.mean(x * x, axis=-1, keepdims=True)
    h_ref[...] = (x * lax.rsqrt(ms + RMS_EPS) * nw_ref[...]).astype(BF16)

    def proj_steps(nb):
        p = nb % 2
        steps = []
        for dst, col in ((xb_ref, nb * bw), (yb_ref, width + nb * bw)):
            acc = {}

            def kdot(kb, col=col, acc=acc):
                def f():
                    ks = pl.ds(kb * MXU_TILE, MXU_TILE)
                    d = _dot(h_ref[:, ks], w_ref[ks, pl.ds(col, bw)])
                    acc["v"] = d if kb == 0 else acc["v"] + d
                return f

            def park(dst=dst, acc=acc):
                def f():
                    dst[p] = acc["v"]
                return f

            steps += [kdot(kb) for kb in range(dm // MXU_TILE)] + [park()]
        return steps

    def gate_steps(nb):
        p = nb % 2
        cs = pl.ds(nb * bw, bw)
        t = {}

        def conv():
            xb = xb_ref[p]
            ext = jnp.concatenate([ccar_ref[:, cs], xb], axis=0)
            ccar_ref[:, cs] = xb[tt - SUBLANES:, :]
            cw = cw_ref[:, cs]
            xc = cw[kw - 1:kw, :] * xb + cb_ref[:, cs]
            for d in range(1, kw):
                xc = xc + cw[kw - 1 - d:kw - d, :] * ext[SUBLANES - d:SUBLANES - d + tt, :]
            t["xc"] = xc

        def gates():
            xc = t["xc"]
            xcb = xc.astype(BF16)
            rg = _sigmoid(_dot(xcb, wga_ref[nb]) + bga_ref[:, cs])
            ig = _sigmoid(_dot(xcb, wgx_ref[nb]) + bgx_ref[:, cs])
            a = jnp.exp(rg * (-LRU_C * _softplus(-lam_ref[:, cs])))
            a_ref[:, cs] = a
            y = 1.0 - a * a
            u_ref[:, cs] = xc * ig * jnp.where(y > 0.0, y * lax.rsqrt(y), 0.0)

        def out_gate():
            gy_ref[:, cs] = _gelu_tanh(yb_ref[p])

        return [conv, gates, out_gate]

    for f in proj_steps(0):
        f()
    for nb in range(n_blocks):
        slot = [gate_steps(nb)] + ([proj_steps(nb + 1)] if nb + 1 < n_blocks else [])
        for f in _interleave(*slot):
            f()

    def step(t, h):
        h = a_ref[pl.ds(t, 1), :] * h + u_ref[pl.ds(t, 1), :]
        hs_ref[pl.ds(t, 1), :] = h
        return h

    hcar_ref[0:1, :] = lax.fori_loop(0, tt, step, hcar_ref[0:1, :], unroll=8)

    for nb in range(n_blocks):
        cs = pl.ds(nb * bw, bw)
        o_ref[:, cs] = (hs_ref[:, cs] * gy_ref[:, cs]).astype(o_ref.dtype)


def _lru_layer(x, nw, w_in, conv_w, conv_b, w_ga, b_ga, w_gx, b_gx, lam, layer, *, batch, seq_len, tt,
               name):
    M, D = x.shape
    width = conv_w.shape[1]
    nt = seq_len // tt
    nb = w_ga.shape[1]
    bw = width // nb
    vec = lambda: pl.BlockSpec((1, width), lambda b, n: (0, 0))
    wblk = lambda: pl.BlockSpec((None,) + w_ga.shape[1:], lambda b, n: (layer, 0, 0, 0))
    return pl.pallas_call(
        functools.partial(_lru_layer_kernel, n_blocks=nb),
        grid=(batch, nt),
        in_specs=[
            pl.BlockSpec((tt, D), lambda b, n: (b * nt + n, 0)),
            pl.BlockSpec((1, D), lambda b, n: (0, 0)),
            pl.BlockSpec((None, D, 2 * width), lambda b, n: (layer, 0, 0),
                         pipeline_mode=pl.Buffered(1)),
            pl.BlockSpec(conv_w.shape, lambda b, n: (0, 0)),
            vec(), wblk(), vec(), wblk(), vec(), vec(),
        ],
        out_specs=pl.BlockSpec((tt, width), lambda b, n: (b * nt + n, 0)),
        out_shape=jax.ShapeDtypeStruct((M, width), BF16),
        scratch_shapes=[
            pltpu.VMEM((tt, D), BF16),
            pltpu.VMEM((SUBLANES, width), F32),
            pltpu.VMEM((SUBLANES, width), F32),
            pltpu.VMEM((2, tt, bw), F32),
            pltpu.VMEM((2, tt, bw), F32),
            pltpu.VMEM((tt, width), F32),
            pltpu.VMEM((tt, width), F32),
            pltpu.VMEM((tt, width), F32),
            pltpu.VMEM((tt, width), F32),
        ],
        compiler_params=_params(("arbitrary", "arbitrary")),
        name=name,
    )(x, nw.reshape(1, D), w_in, conv_w, conv_b.reshape(1, width), w_ga, b_ga.reshape(1, width),
      w_gx, b_gx.reshape(1, width), lam.reshape(1, width))


def kernel(x, norm_mix_w, norm_ffn_w, norm_out_w, gla_w_in, gla_w_gk, gla_b_gk, gla_norm_w, gla_w_out, lru_w_in, lru_conv_w, lru_conv_b, lru_w_ga, lru_b_ga, lru_w_gx, lru_b_gx, lru_lambda, lru_w_out, ret_w_in, ret_norm_w, ret_w_out, ffn_w_up, ffn_conv_w, ffn_conv_b, ffn_w_down):
    B, S, D = x.shape
    M = B * S
    depth = norm_mix_w.shape[0]
    xf = x.reshape(M, D)

    gla_kd = gla_w_gk.shape[2]
    gla_vd = gla_norm_w.shape[1]
    gla_main = 2 * gla_kd + 2 * gla_vd
    rank = gla_w_gk.shape[1]
    ret_dk = ret_w_in.shape[2] // (6 * RET_HEADS)
    cos = sin = None
    tm_p = min(1024, M)
    tm_f = min(512, S)
    tc = min(256, S)
    tt_lru = min(512, S)
    tc_ret = min(512, S)

    gla_w_in_b = jnp.pad(gla_w_in, ((0, 0), (0, 0), (0, LANES - rank))).astype(BF16)
    gla_w_out_b = gla_w_out.astype(BF16)
    lru_w_in_b, lru_w_out_b = lru_w_in.astype(BF16), lru_w_out.astype(BF16)
    lru_w_ga_b, lru_w_gx_b = lru_w_ga.astype(BF16), lru_w_gx.astype(BF16)
    ret_w_in_b, ret_w_out_b = ret_w_in.astype(BF16), ret_w_out.astype(BF16)
    ffn_w_up_b, ffn_w_down_b = ffn_w_up[:1].astype(BF16), ffn_w_down[:1].astype(BF16)

    for i in range(depth):
        kind, l = i % N_MIXERS, i // N_MIXERS
        if kind == 0:
            w_gk = jnp.pad(gla_w_gk[l].astype(BF16), ((0, LANES - rank), (0, 0)))
            y = _gla_layer(xf.reshape(B, S, D), norm_mix_w[i], gla_w_in_b, l, w_gk, gla_b_gk[l],
                           gla_norm_w[l], tc=tc, name=f"gla_mix_{i}")
            xf = _proj_res(y.reshape(M, gla_vd), gla_w_out_b, l, xf, tm=tm_p, tn=1024,
                           name=f"gla_out_{i}")
        elif kind == 1:
            y = _lru_layer(xf, norm_mix_w[i], lru_w_in_b, lru_conv_w[l], lru_conv_b[l], lru_w_ga_b,
                           lru_b_ga[l], lru_w_gx_b, lru_b_gx[l], lru_lambda[l], l,
                           batch=B, seq_len=S, tt=tt_lru, name=f"lru_mix_{i}")
            xf = _proj_res(y, lru_w_out_b, l, xf, tm=tm_p, tn=1024, name=f"lru_out_{i}")
        else:
            if cos is None:
                cos, sin = _rope_tables(S, ret_dk // 2, tt=tm_f)
            y = _ret_layer(xf.reshape(B, S, D), norm_mix_w[i], ret_w_in_b, l, cos, sin, ret_norm_w[l],
                           tc=tc, heads_per_step=2, name=f"ret_mix_{i}")
            xf = _proj_res(y.reshape(M, ret_w_out.shape[1]), ret_w_out_b, l, xf, tm=tm_p, tn=512,
                           name=f"ret_out_{i}")
        last = i == depth - 1
        res = _ffn(xf, norm_ffn_w[i], ffn_w_up_b, ffn_conv_w[i], ffn_conv_b[i], ffn_w_down_b, 0,
                   norm_out_w if last else None, None if last else (ffn_w_up, ffn_w_down, i + 1),
                   seq_len=S, tm=tm_f, tf=1024, sub=256, name=f"ffn_{i}")
        if last:
            xf = res
        else:
            xf, ffn_w_up_b, ffn_w_down_b = res
    return xf.reshape(B, S, D)
```
